```python
import jax, jax.numpy as jnp
from jax import lax
import numpy as np

D_MODEL = 1024
BATCH = 4
SEQ = 4096
DEPTH = 1
DEC_BATCH = 16
DEC_SEQ = 64
PAST_LEN = 1024

CHUNK = 64
HG_HEADS = 8
HG_DK = 128
HG_DV = D_MODEL // HG_HEADS
HG_F = HG_HEADS * HG_DK
HG_V = HG_HEADS * HG_DV
CONV_DIM = 1024
CONV_W = 3
D_FF = ((8 * D_MODEL // 3 + 255) // 256) * 256
PLE_DIM = 256
EPS = 1e-6
IN_SPLITS = (HG_F, 2 * HG_F, 2 * HG_F + HG_V, 2 * HG_F + 2 * HG_V,
             2 * HG_F + 2 * HG_V + CONV_DIM, 2 * HG_F + 2 * HG_V + 2 * CONV_DIM,
             2 * HG_F + 2 * HG_V + 3 * CONV_DIM, 2 * HG_F + 2 * HG_V + 3 * CONV_DIM + D_MODEL)
IN_COLS = 2 * HG_F + 2 * HG_V + 3 * CONV_DIM + 2 * D_MODEL

kernel_name = 'hgrn2_shortconv_gated_stream_step'


def rmsnorm(x, g):
    xf = x.astype(jnp.float32)
    y = xf * lax.rsqrt(jnp.mean(xf * xf, axis=-1, keepdims=True) + EPS)
    return (y * g.astype(jnp.float32)).astype(x.dtype)


def hgrn_block(S, q, k, v, logf):
    c = q.shape[2]
    bcum = jnp.cumsum(logf, axis=2)
    causal = jnp.tril(jnp.ones((c, c), dtype=bool))[None, None, :, :, None]
    diff = bcum[:, :, :, None, :] - bcum[:, :, None, :, :]
    decay = jnp.exp(jnp.where(causal, diff, -jnp.inf))
    scores = jnp.einsum('bhtk,bhtsk,bhsk->bhts', q, decay, k)
    o = (jnp.einsum('bhts,bhsv->bhtv', scores, v)
         + jnp.einsum('bhtk,bhkv->bhtv', q * jnp.exp(bcum), S))
    blast = bcum[:, :, -1:, :]
    S_new = (jnp.exp(blast[:, :, 0, :])[..., None] * S
             + jnp.einsum('bhsk,bhsv->bhkv', k * jnp.exp(blast - bcum), v))
    return S_new, o


def hgrn2(q_raw, f_raw, i_raw, g_raw, lb, S0, g_norm):
    b, t, _ = q_raw.shape
    c = min(CHUNK, t)
    n = t // c

    def blocks(a, d):
        return a.astype(jnp.float32).reshape(b, n, c, HG_HEADS, d).transpose(1, 0, 3, 2, 4)

    fz = f_raw.astype(jnp.float32)
    logf = jnp.log(lb + (1.0 - lb) * jax.nn.sigmoid(fz))
    k = (1.0 - lb) * jax.nn.sigmoid(-fz)
    q = jax.nn.silu(q_raw.astype(jnp.float32)) * HG_DK ** -0.5
    xs = (blocks(q, HG_DK), blocks(k, HG_DK), blocks(i_raw, HG_DV), blocks(logf, HG_DK))

    def step(S, blk):
        return hgrn_block(S, *blk)

    S_fin, o = lax.scan(step, S0.astype(jnp.float32), xs)
    o = o.transpose(1, 0, 3, 2, 4).reshape(b, t, HG_HEADS, HG_DV)
    o = rmsnorm(o, g_norm).reshape(b, t, HG_V)
    return (o * jax.nn.silu(g_raw.astype(jnp.float32))).astype(q_raw.dtype), S_fin


def short_conv(u, buf, w):
    t = u.shape[1]
    full = jnp.concatenate([buf.astype(u.dtype), u], axis=1)
    y = w[0] * full[:, 0:t]
    for j in range(1, CONV_W):
        y = y + w[j] * full[:, j:j + t]
    return y, full[:, t:]


def layer(x, p, S0, buf, lb, norm_mix, w_in, conv_w, hg_norm, w_branch_a, w_branch_b,
          w_out, norm_ffn, w_gate_up, w_down, norm_ple, w_ple, w_ple_gate):
    n = rmsnorm(x, norm_mix)
    z = n @ w_in
    q_raw, f_raw, i_raw, g_raw, b_g, c_g, h_c, za, zb = jnp.split(z, IN_SPLITS, axis=-1)
    o_a, S_new = hgrn2(q_raw, f_raw, i_raw, g_raw, lb, S0, hg_norm)
    conv_out, buf_new = short_conv(c_g * h_c, buf, conv_w)
    o_b = b_g * conv_out
    mix = jax.nn.sigmoid(za) * (o_a @ w_branch_a) + jax.nn.sigmoid(zb) * (o_b @ w_branch_b)
    x = x + mix @ w_out
    gate, up = jnp.split(rmsnorm(x, norm_ffn) @ w_gate_up, 2, axis=-1)
    x = x + (jax.nn.silu(gate) * up) @ w_down
    x = x + jax.nn.sigmoid(rmsnorm(x, norm_ple) @ w_ple_gate) * (p @ w_ple)
    return x, S_new, buf_new


def setup_inputs(seed: int = 0) -> dict:
    key = jax.random.key(seed)
    ks = jax.random.split(key, 21)

    def nrm(k, shape, scale):
        return jax.random.normal(k, shape, jnp.float32) * scale

    return {
        'x_prompt': nrm(ks[0], (BATCH, SEQ, D_MODEL), 1.0),
        'x_sample': nrm(ks[1], (DEC_BATCH, DEC_SEQ, D_MODEL), 1.0),
        'p_prompt': nrm(ks[2], (DEPTH, BATCH, SEQ, PLE_DIM), 1.0),
        'p_sample': nrm(ks[3], (DEPTH, DEC_BATCH, DEC_SEQ, PLE_DIM), 1.0),
        'state_hgrn': nrm(ks[4], (DEPTH, DEC_BATCH, HG_HEADS, HG_DK, HG_DV), 0.5),
        'state_conv': nrm(ks[5], (DEPTH, DEC_BATCH, CONV_W - 1, CONV_DIM), 1.0),
        'lower_bounds': nrm(ks[6], (DEPTH + 1, HG_F), 1.0),
        'norm_mix': 1.0 + nrm(ks[7], (DEPTH, D_MODEL), 0.01),
        'w_in': nrm(ks[8], (DEPTH, D_MODEL, IN_COLS), D_MODEL ** -0.5),
        'conv_w': nrm(ks[9], (DEPTH, CONV_W, CONV_DIM), CONV_W ** -0.5),
        'hg_norm': 1.0 + nrm(ks[10], (DEPTH, HG_DV), 0.01),
        'w_branch_a': nrm(ks[11], (DEPTH, HG_V, D_MODEL), HG_V ** -0.5),
        'w_branch_b': nrm(ks[12], (DEPTH, CONV_DIM, D_MODEL), CONV_DIM ** -0.5),
        'w_out': nrm(ks[13], (DEPTH, D_MODEL, D_MODEL), D_MODEL ** -0.5),
        'norm_ffn': 1.0 + nrm(ks[14], (DEPTH, D_MODEL), 0.01),
        'w_gate_up': nrm(ks[15], (DEPTH, D_MODEL, 2 * D_FF), D_MODEL ** -0.5),
        'w_down': nrm(ks[16], (DEPTH, D_FF, D_MODEL), D_FF ** -0.5),
        'norm_ple': 1.0 + nrm(ks[17], (DEPTH, D_MODEL), 0.01),
        'w_ple': nrm(ks[18], (DEPTH, PLE_DIM, D_MODEL), PLE_DIM ** -0.5),
        'w_ple_gate': nrm(ks[19], (DEPTH, D_MODEL, D_MODEL), D_MODEL ** -0.5),
        'norm_final': 1.0 + nrm(ks[20], (D_MODEL,), 0.01),
    }


def reference(x_prompt, x_sample, p_prompt, p_sample, state_hgrn, state_conv, lower_bounds,
              norm_mix, w_in, conv_w, hg_norm, w_branch_a, w_branch_b, w_out, norm_ffn,
              w_gate_up, w_down, norm_ple, w_ple, w_ple_gate, norm_final):
    lb_all = jnp.cumsum(jax.nn.softmax(lower_bounds.astype(jnp.float32), axis=0), axis=0)
    b = x_prompt.shape[0]
    hp, hs = x_prompt, x_sample
    hgrn_p, conv_p, hgrn_s, conv_s = [], [], [], []
    for l in range(DEPTH):
        w = (lb_all[l], norm_mix[l], w_in[l], conv_w[l], hg_norm[l], w_branch_a[l], w_branch_b[l],
             w_out[l], norm_ffn[l], w_gate_up[l], w_down[l], norm_ple[l], w_ple[l], w_ple_gate[l])
        S0 = jnp.zeros((b, HG_HEADS, HG_DK, HG_DV), jnp.float32)
        buf0 = jnp.zeros((b, CONV_W - 1, CONV_DIM), x_prompt.dtype)
        hp, Sp, cp = layer(hp, p_prompt[l], S0, buf0, *w)
        hs, Ss, cs = layer(hs, p_sample[l], state_hgrn[l], state_conv[l], *w)
        hgrn_p.append(Sp.astype(state_hgrn.dtype))
        conv_p.append(cp.astype(state_conv.dtype))
        hgrn_s.append(Ss.astype(state_hgrn.dtype))
        conv_s.append(cs.astype(state_conv.dtype))
    y_prompt = rmsnorm(hp, norm_final)
    y_sample = rmsnorm(hs, norm_final)
    return (y_prompt, y_sample, jnp.stack(hgrn_p), jnp.stack(conv_p), jnp.stack(hgrn_s), jnp.stack(conv_s))
```

```python
import functools

import jax
import jax.numpy as jnp
from jax import lax
from jax.experimental import pallas as pl
from jax.experimental.pallas import tpu as pltpu

D_MODEL = 1024
HEADS = 8
HEAD_DIM = 128
CONV_W = 3
D_FF = 2816
PLE_DIM = 256
EPS = 1e-6
N_SECTIONS = 9
SUBLANES = 8
TOKEN_BLOCK = 256
VMEM_LIMIT_BYTES = 56 * 1024 * 1024

F32 = jnp.float32
BF16 = jnp.bfloat16


def _dot(a, b):
    return jnp.dot(a, b, preferred_element_type=F32)


def _dot_nt(a, b):
    return lax.dot_general(a, b, (((1,), (1,)), ((), ())), preferred_element_type=F32)


def _dot_tn(a, b):
    return lax.dot_general(a, b, (((0,), (0,)), ((), ())), preferred_element_type=F32)


def _rmsnorm(x, g):
    ms = jnp.mean(x * x, axis=-1, keepdims=True)
    return x * lax.rsqrt(ms + EPS) * g


def _sigmoid(x):
    return 1.0 / (1.0 + jnp.exp(-x))


def _level_step(c, blk):
    t, lanes = c.shape
    if blk >= SUBLANES:
        g = t // (2 * blk)
        c4 = c.reshape(g, 2, blk, lanes)
        ev = c4[:, 0]
        od = c4[:, 1]
        r = ev[:, blk - 1:blk, :]
        a = jnp.stack([r - ev, od], axis=1).reshape(t, lanes)
        cn = jnp.stack([ev, od + r], axis=1).reshape(t, lanes)
        return a, cn
    c3 = c.reshape(t // SUBLANES, SUBLANES, lanes)
    sub = lax.broadcasted_iota(jnp.int32, c3.shape, 1)
    r = None
    for grp in range(SUBLANES // (2 * blk)):
        row = grp * 2 * blk + blk - 1
        rg = jnp.broadcast_to(c3[:, row:row + 1, :], c3.shape)
        r = rg if r is None else jnp.where(sub >= grp * 2 * blk, rg, r)
    odd = (sub & blk) != 0
    a = jnp.where(odd, c3, r - c3).reshape(t, lanes)
    cn = jnp.where(odd, c3 + r, c3).reshape(t, lanes)
    return a, cn


def _level_codes(t, seg):
    row = lax.broadcasted_iota(jnp.int32, (t, t), 0)
    col = lax.broadcasted_iota(jnp.int32, (t, t), 1)
    x = row ^ col
    cnt = jnp.zeros((t, t), jnp.int32)
    blk = 1
    while blk < seg:
        cnt = cnt + jnp.where(x >= blk, 1, 0)
        blk *= 2
    lower = jnp.where(row > col, jnp.where(x < seg, cnt, -1), -1)
    return jnp.where(row == col, 0, lower)


def _mixer_kernel(*refs, t, nseq, has_init, nj):
    if has_init:
        (x_ref, s0_ref, c0_ref, lbw_ref, nmix_ref, win_ref, convw_ref, hgn_ref, wa_ref, wb_ref,
         wout_ref, y_ref, sout_ref, cout_ref,
         n_scr, q_scr, k_scr, lf_scr, v_scr, sg_scr, oa_scr, st_scr, code_scr) = refs
    else:
        (x_ref, lbw_ref, nmix_ref, win_ref, convw_ref, hgn_ref, wa_ref, wb_ref,
         wout_ref, y_ref, sout_ref, cout_ref,
         n_scr, q_scr, k_scr, lf_scr, v_scr, sg_scr, oa_scr, st_scr, code_scr) = refs
        s0_ref = c0_ref = None
    seg = t // nseq
    b = pl.program_id(0)
    j = pl.program_id(1)

    @pl.when((b == 0) & (j == 0))
    def _():
        code_scr[...] = _level_codes(t, seg)

    @pl.when(j == 0)
    def _():
        for sq in range(nseq):
            for h in range(HEADS):
                if has_init:
                    st_scr[sq * HEADS + h] = s0_ref[0, sq, h].T
                else:
                    st_scr[sq * HEADS + h] = jnp.zeros((HEAD_DIM, HEAD_DIM), F32)
        if has_init:
            cout_ref[...] = c0_ref[...]
        else:
            cout_ref[...] = jnp.zeros(cout_ref.shape, F32)

    x = x_ref[0]
    n_scr[...] = _rmsnorm(x, nmix_ref[...]).astype(BF16)

    def proj(sec):
        return _dot(n_scr[...], win_ref[:, sec * D_MODEL:(sec + 1) * D_MODEL])

    lbw = lbw_ref[...]
    mx = lbw[0:1]
    for r in range(1, lbw.shape[0]):
        mx = jnp.maximum(mx, lbw[r:r + 1])
    den = jnp.zeros_like(mx)
    for r in range(lbw.shape[0]):
        den = den + jnp.exp(lbw[r:r + 1] - mx)
    lb = jnp.exp(lbw[0:1] - mx) / den

    zq = proj(0)
    q_all = zq * _sigmoid(zq) * (HEAD_DIM ** -0.5)
    zf = proj(1)
    sgf = _sigmoid(zf)
    lf_all = jnp.log(lb + (1.0 - lb) * sgf)
    k_all = (1.0 - lb) * (1.0 - sgf)
    zi = proj(2)
    zg = proj(3)
    sg_all = zg * _sigmoid(zg)
    for h in range(HEADS):
        cols = slice(h * HEAD_DIM, (h + 1) * HEAD_DIM)
        q_scr[h] = q_all[:, cols]
        k_scr[h] = k_all[:, cols]
        lf_scr[h] = lf_all[:, cols]
        v_scr[h] = zi[:, cols].astype(BF16)
        sg_scr[h] = sg_all[:, cols]

    hgn = hgn_ref[...]

    def head_body(h, carry):
        q = q_scr[h]
        k = k_scr[h]
        v = v_scr[h]
        code = code_scr[...]
        p = jnp.where(code == 0, _dot_nt(q.astype(BF16), k.astype(BF16)), 0.0)
        c = lf_scr[h]
        blk, lvl = 1, 1
        while blk < seg:
            a, c = _level_step(c, blk)
            xs = jnp.exp(a)
            s = _dot_nt((q * xs).astype(BF16), (k * xs).astype(BF16))
            p = jnp.where(code == lvl, s, p)
            blk *= 2
            lvl += 1
        o = _dot(p.astype(BF16), v)
        inter = []
        for sq in range(nseq):
            rows = slice(sq * seg, (sq + 1) * seg)
            cs = c[rows]
            tot = cs[seg - 1:seg]
            st = st_scr[sq * HEADS + h]
            qd = (q[rows] * jnp.exp(cs)).astype(BF16)
            inter.append(_dot_nt(qd, st.astype(BF16)))
            kd = (k[rows] * jnp.exp(tot - cs)).astype(BF16)
            st_scr[sq * HEADS + h] = st * jnp.exp(tot) + _dot_tn(v[rows], kd)
        o = o + (inter[0] if nseq == 1 else jnp.concatenate(inter, axis=0))
        on = _rmsnorm(o, hgn)
        oa_scr[h] = (on * sg_scr[h]).astype(BF16)
        return carry

    lax.fori_loop(0, HEADS, head_body, 0)

    bg = proj(4)
    u = proj(5) * proj(6)
    cw = convw_ref[...]
    rowi = lax.broadcasted_iota(jnp.int32, (seg, D_MODEL), 0)
    conv_parts = []
    for sq in range(nseq):
        us = u[sq * seg:(sq + 1) * seg]
        prev = cout_ref[0, sq]
        s1 = jnp.where(rowi == 0, prev[1:2], pltpu.roll(us, 1, 0))
        s2 = jnp.where(rowi == 0, prev[0:1], jnp.where(rowi == 1, prev[1:2], pltpu.roll(us, 2, 0)))
        conv_parts.append(cw[0:1] * s2 + cw[1:2] * s1 + cw[2:3] * us)
        cout_ref[0, sq] = us[seg - (CONV_W - 1):seg]
    conv = conv_parts[0] if nseq == 1 else jnp.concatenate(conv_parts, axis=0)
    ob = (bg * conv).astype(BF16)

    oa = jnp.concatenate([oa_scr[h] for h in range(HEADS)], axis=1)
    mix = _sigmoid(proj(7)) * _dot(oa, wa_ref[...]) + _sigmoid(proj(8)) * _dot(ob, wb_ref[...])
    y_ref[0] = x + _dot(mix.astype(BF16), wout_ref[...])

    @pl.when(j == nj - 1)
    def _():
        for sq in range(nseq):
            for h in range(HEADS):
                sout_ref[0, sq, h] = st_scr[sq * HEADS + h].T


def _ffn_kernel(x_ref, p_ref, nffn_ref, wgu_ref, wdown_ref, nple_ref, wple_ref, wpg_ref, nfin_ref,
                y_ref):
    x = x_ref[0]
    n = _rmsnorm(x, nffn_ref[...]).astype(BF16)
    gate = _dot(n, wgu_ref[:, 0:D_FF])
    up = _dot(n, wgu_ref[:, D_FF:2 * D_FF])
    hidden = (gate * _sigmoid(gate) * up).astype(BF16)
    x = x + _dot(hidden, wdown_ref[...])
    n2 = _rmsnorm(x, nple_ref[...]).astype(BF16)
    ple = _dot(p_ref[0].astype(BF16), wple_ref[...])
    x = x + _sigmoid(_dot(n2, wpg_ref[...])) * ple
    y_ref[0] = _rmsnorm(x, nfin_ref[...])


def _const_spec(shape):
    zeros = (0,) * len(shape)
    return pl.BlockSpec(shape, lambda b, j: zeros, pipeline_mode=pl.Buffered(1))


def _mixer_call(x, s0, c0, weights, *, nseq):
    nb, tokens, _ = x.shape
    t = TOKEN_BLOCK
    nj = tokens // t
    has_init = s0 is not None
    assert nseq == 1 or nj == 1
    state_spec = pl.BlockSpec((1, nseq, HEADS, HEAD_DIM, HEAD_DIM), lambda b, j: (b, 0, 0, 0, 0))
    conv_spec = pl.BlockSpec((1, nseq, CONV_W - 1, D_MODEL), lambda b, j: (b, 0, 0, 0))
    x_spec = pl.BlockSpec((1, t, D_MODEL), lambda b, j: (b, j, 0))
    in_specs = [x_spec]
    args = [x]
    if has_init:
        in_specs += [state_spec, conv_spec]
        args += [s0, c0]
    for w in weights:
        in_specs.append(_const_spec(w.shape))
        args.append(w)
    kern = functools.partial(_mixer_kernel, t=t, nseq=nseq, has_init=has_init, nj=nj)
    return pl.pallas_call(
        kern,
        grid=(nb, nj),
        in_specs=in_specs,
        out_specs=[x_spec, state_spec, conv_spec],
        out_shape=[
            jax.ShapeDtypeStruct(x.shape, F32),
            jax.ShapeDtypeStruct((nb, nseq, HEADS, HEAD_DIM, HEAD_DIM), F32),
            jax.ShapeDtypeStruct((nb, nseq, CONV_W - 1, D_MODEL), F32),
        ],
        scratch_shapes=[
            pltpu.VMEM((t, D_MODEL), BF16),
            pltpu.VMEM((HEADS, t, HEAD_DIM), F32),
            pltpu.VMEM((HEADS, t, HEAD_DIM), F32),
            pltpu.VMEM((HEADS, t, HEAD_DIM), F32),
            pltpu.VMEM((HEADS, t, HEAD_DIM), BF16),
            pltpu.VMEM((HEADS, t, HEAD_DIM), F32),
            pltpu.VMEM((HEADS, t, HEAD_DIM), BF16),
            pltpu.VMEM((nseq * HEADS, HEAD_DIM, HEAD_DIM), F32),
            pltpu.VMEM((t, t), jnp.int32),
        ],
        compiler_params=pltpu.CompilerParams(
            dimension_semantics=("arbitrary", "arbitrary"),
            vmem_limit_bytes=VMEM_LIMIT_BYTES),
        name="mixer_init" if has_init else "mixer_zero",
    )(*args)


def _ffn_call(x, p, weights, name):
    nb, tokens, _ = x.shape
    t = TOKEN_BLOCK
    nj = tokens // t
    x_spec = pl.BlockSpec((1, t, D_MODEL), lambda b, j: (b, j, 0))
    p_spec = pl.BlockSpec((1, t, PLE_DIM), lambda b, j: (b, j, 0))
    return pl.pallas_call(
        _ffn_kernel,
        grid=(nb, nj),
        in_specs=[x_spec, p_spec] + [_const_spec(w.shape) for w in weights],
        out_specs=x_spec,
        out_shape=jax.ShapeDtypeStruct(x.shape, F32),
        compiler_params=pltpu.CompilerParams(
            dimension_semantics=("arbitrary", "arbitrary"),
            vmem_limit_bytes=VMEM_LIMIT_BYTES),
        name=name,
    )(x, p, *weights)


def kernel(x_prompt, x_sample, p_prompt, p_sample, state_hgrn, state_conv, lower_bounds, norm_mix, w_in, conv_w, hg_norm, w_branch_a, w_branch_b, w_out, norm_ffn, w_gate_up, w_down, norm_ple, w_ple, w_ple_gate, norm_final):
    depth = w_in.shape[0]
    assert depth == 1
    batch, seq, _ = x_prompt.shape
    dec_batch, dec_seq, _ = x_sample.shape
    assert seq % TOKEN_BLOCK == 0 and TOKEN_BLOCK % dec_seq == 0
    spb = TOKEN_BLOCK // dec_seq
    assert dec_batch % spb == 0
    nsb = dec_batch // spb

    row = lambda a: a.reshape(1, -1)
    mixer_w = [lower_bounds, row(norm_mix[0]), w_in[0].astype(BF16), conv_w[0], row(hg_norm[0]),
               w_branch_a[0].astype(BF16), w_branch_b[0].astype(BF16), w_out[0].astype(BF16)]
    ffn_w = [row(norm_ffn[0]), w_gate_up[0].astype(BF16), w_down[0].astype(BF16), row(norm_ple[0]),
             w_ple[0].astype(BF16), w_ple_gate[0].astype(BF16), row(norm_final)]

    hp, sp, cp = _mixer_call(x_prompt, None, None, mixer_w, nseq=1)
    xs = x_sample.reshape(nsb, TOKEN_BLOCK, D_MODEL)
    s0 = state_hgrn[0].reshape(nsb, spb, HEADS, HEAD_DIM, HEAD_DIM)
    c0 = state_conv[0].reshape(nsb, spb, CONV_W - 1, D_MODEL)
    hs, ss, cs = _mixer_call(xs, s0, c0, mixer_w, nseq=spb)

    y_prompt = _ffn_call(hp, p_prompt[0], ffn_w, "ffn_prompt")
    y_sample = _ffn_call(hs, p_sample[0].reshape(nsb, TOKEN_BLOCK, PLE_DIM), ffn_w, "ffn_sample")

    return (y_prompt,
            y_sample.reshape(dec_batch, dec_seq, D_MODEL),
            sp.reshape(1, batch, HEADS, HEAD_DIM, HEAD_DIM),
            cp.reshape(1, batch, CONV_W - 1, D_MODEL),
            ss.reshape(1, dec_batch, HEADS, HEAD_DIM, HEAD_DIM),
            cs.reshape(1, dec_batch, CONV_W - 1, D_MODEL))
```

```python
import functools

import jax
import jax.numpy as jnp
from jax import lax
from jax.experimental import pallas as pl
from jax.experimental.pallas import tpu as pltpu

D_MODEL = 1024
HEADS = 8
HEAD_DIM = 128
CONV_W = 3
D_FF = 2816
PLE_DIM = 256
EPS = 1e-6
N_SECTIONS = 9
SUBLANES = 8
TOKEN_BLOCK = 256
VMEM_LIMIT_BYTES = 56 * 1024 * 1024

F32 = jnp.float32
BF16 = jnp.bfloat16


def _dot(a, b):
    return jnp.dot(a, b, preferred_element_type=F32)


def _dot_nt(a, b):
    return lax.dot_general(a, b, (((1,), (1,)), ((), ())), preferred_element_type=F32)


def _dot_tn(a, b):
    return lax.dot_general(a, b, (((0,), (0,)), ((), ())), preferred_element_type=F32)


def _rmsnorm(x, g):
    ms = jnp.mean(x * x, axis=-1, keepdims=True)
    return x * lax.rsqrt(ms + EPS) * g


def _sigmoid(x):
    return 1.0 / (1.0 + jnp.exp(-x))


def _level_step(c, blk):
    t, lanes = c.shape
    if blk >= SUBLANES:
        g = t // (2 * blk)
        c4 = c.reshape(g, 2, blk, lanes)
        ev = c4[:, 0]
        od = c4[:, 1]
        r = ev[:, blk - 1:blk, :]
        a = jnp.stack([r - ev, od], axis=1).reshape(t, lanes)
        cn = jnp.stack([ev, od + r], axis=1).reshape(t, lanes)
        return a, cn
    c3 = c.reshape(t // SUBLANES, SUBLANES, lanes)
    sub = lax.broadcasted_iota(jnp.int32, c3.shape, 1)
    r = None
    for grp in range(SUBLANES // (2 * blk)):
        row = grp * 2 * blk + blk - 1
        rg = jnp.broadcast_to(c3[:, row:row + 1, :], c3.shape)
        r = rg if r is None else jnp.where(sub >= grp * 2 * blk, rg, r)
    odd = (sub & blk) != 0
    a = jnp.where(odd, c3, r - c3).reshape(t, lanes)
    cn = jnp.where(odd, c3 + r, c3).reshape(t, lanes)
    return a, cn


def _level_codes(t, seg):
    row = lax.broadcasted_iota(jnp.int32, (t, t), 0)
    col = lax.broadcasted_iota(jnp.int32, (t, t), 1)
    x = row ^ col
    cnt = jnp.zeros((t, t), jnp.int32)
    blk = 1
    while blk < seg:
        cnt = cnt + jnp.where(x >= blk, 1, 0)
        blk *= 2
    lower = jnp.where(row > col, jnp.where(x < seg, cnt, -1), -1)
    return jnp.where(row == col, 0, lower)


def _mixer_kernel(*refs, t, nseq, has_init, nj):
    if has_init:
        (x_ref, s0_ref, c0_ref, lbw_ref, nmix_ref, win_ref, convw_ref, hgn_ref, wa_ref, wb_ref,
         wout_ref, y_ref, sout_ref, cout_ref,
         n_scr, st_scr, code_scr) = refs
    else:
        (x_ref, lbw_ref, nmix_ref, win_ref, convw_ref, hgn_ref, wa_ref, wb_ref,
         wout_ref, y_ref, sout_ref, cout_ref,
         n_scr, st_scr, code_scr) = refs
        s0_ref = c0_ref = None
    seg = t // nseq
    b = pl.program_id(0)
    j = pl.program_id(1)

    @pl.when((b == 0) & (j == 0))
    def _():
        code_scr[...] = _level_codes(t, seg)

    @pl.when(j == 0)
    def _():
        for sq in range(nseq):
            for h in range(HEADS):
                if has_init:
                    st_scr[sq * HEADS + h] = s0_ref[0, sq, h].T
                else:
                    st_scr[sq * HEADS + h] = jnp.zeros((HEAD_DIM, HEAD_DIM), F32)
        if has_init:
            cout_ref[...] = c0_ref[...]
        else:
            cout_ref[...] = jnp.zeros(cout_ref.shape, F32)

    x = x_ref[0]
    n_scr[...] = _rmsnorm(x, nmix_ref[...]).astype(BF16)

    def proj(sec):
        return _dot(n_scr[...], win_ref[:, sec * D_MODEL:(sec + 1) * D_MODEL])

    lbw = lbw_ref[...]
    mx = lbw[0:1]
    for r in range(1, lbw.shape[0]):
        mx = jnp.maximum(mx, lbw[r:r + 1])
    den = jnp.zeros_like(mx)
    for r in range(lbw.shape[0]):
        den = den + jnp.exp(lbw[r:r + 1] - mx)
    lb = jnp.exp(lbw[0:1] - mx) / den

    hgn = hgn_ref[...]
    code = code_scr[...]
    pair = 2 * HEAD_DIM

    def proj_cols(sec, c0):
        base = sec * D_MODEL + c0
        return _dot(n_scr[...], win_ref[:, base:base + pair])

    def head(h, q, k, lf, v, sg):
        p = jnp.where(code == 0, _dot_nt(q.astype(BF16), k.astype(BF16)), 0.0)
        c = lf
        blk, lvl = 1, 1
        while blk < seg:
            a, c = _level_step(c, blk)
            xs = jnp.exp(a)
            s = _dot_nt((q * xs).astype(BF16), (k * xs).astype(BF16))
            p = jnp.where(code == lvl, s, p)
            blk *= 2
            lvl += 1
        o = _dot(p.astype(BF16), v)
        inter = []
        for sq in range(nseq):
            rows = slice(sq * seg, (sq + 1) * seg)
            cs = c[rows]
            tot = cs[seg - 1:seg]
            st = st_scr[sq * HEADS + h]
            qd = (q[rows] * jnp.exp(cs)).astype(BF16)
            inter.append(_dot_nt(qd, st.astype(BF16)))
            kd = (k[rows] * jnp.exp(tot - cs)).astype(BF16)
            st_scr[sq * HEADS + h] = st * jnp.exp(tot) + _dot_tn(v[rows], kd)
        o = o + (inter[0] if nseq == 1 else jnp.concatenate(inter, axis=0))
        return (_rmsnorm(o, hgn) * sg).astype(BF16)

    oa_parts = []
    for pr in range(HEADS // 2):
        c0 = pr * pair
        zq = proj_cols(0, c0)
        q2 = zq * _sigmoid(zq) * (HEAD_DIM ** -0.5)
        sgf = _sigmoid(proj_cols(1, c0))
        lbp = lb[:, c0:c0 + pair]
        lf2 = jnp.log(lbp + (1.0 - lbp) * sgf)
        k2 = (1.0 - lbp) * (1.0 - sgf)
        v2 = proj_cols(2, c0).astype(BF16)
        zg = proj_cols(3, c0)
        sg2 = zg * _sigmoid(zg)
        for hh in range(2):
            cols = slice(hh * HEAD_DIM, (hh + 1) * HEAD_DIM)
            oa_parts.append(head(2 * pr + hh, q2[:, cols], k2[:, cols], lf2[:, cols], v2[:, cols],
                                 sg2[:, cols]))

    bg = proj(4)
    u = proj(5) * proj(6)
    cw = convw_ref[...]
    rowi = lax.broadcasted_iota(jnp.int32, (seg, D_MODEL), 0)
    conv_parts = []
    for sq in range(nseq):
        us = u[sq * seg:(sq + 1) * seg]
        prev = cout_ref[0, sq]
        s1 = jnp.where(rowi == 0, prev[1:2], pltpu.roll(us, 1, 0))
        s2 = jnp.where(rowi == 0, prev[0:1], jnp.where(rowi == 1, prev[1:2], pltpu.roll(us, 2, 0)))
        conv_parts.append(cw[0:1] * s2 + cw[1:2] * s1 + cw[2:3] * us)
        cout_ref[0, sq] = us[seg - (CONV_W - 1):seg]
    conv = conv_parts[0] if nseq == 1 else jnp.concatenate(conv_parts, axis=0)
    ob = (bg * conv).astype(BF16)

    oa = jnp.concatenate(oa_parts, axis=1)
    mix = _sigmoid(proj(7)) * _dot(oa, wa_ref[...]) + _sigmoid(proj(8)) * _dot(ob, wb_ref[...])
    y_ref[0] = x + _dot(mix.astype(BF16), wout_ref[...])

    @pl.when(j == nj - 1)
    def _():
        for sq in range(nseq):
            for h in range(HEADS):
                sout_ref[0, sq, h] = st_scr[sq * HEADS + h].T


def _ffn_kernel(x_ref, p_ref, nffn_ref, wgu_ref, wdown_ref, nple_ref, wple_ref, wpg_ref, nfin_ref,
                y_ref):
    x = x_ref[0]
    n = _rmsnorm(x, nffn_ref[...]).astype(BF16)
    gate = _dot(n, wgu_ref[:, 0:D_FF])
    up = _dot(n, wgu_ref[:, D_FF:2 * D_FF])
    hidden = (gate * _sigmoid(gate) * up).astype(BF16)
    x = x + _dot(hidden, wdown_ref[...])
    n2 = _rmsnorm(x, nple_ref[...]).astype(BF16)
    ple = _dot(p_ref[0].astype(BF16), wple_ref[...])
    x = x + _sigmoid(_dot(n2, wpg_ref[...])) * ple
    y_ref[0] = _rmsnorm(x, nfin_ref[...])


def _const_spec(shape):
    zeros = (0,) * len(shape)
    return pl.BlockSpec(shape, lambda b, j: zeros, pipeline_mode=pl.Buffered(1))


def _mixer_call(x, s0, c0, weights, *, nseq):
    nb, tokens, _ = x.shape
    t = TOKEN_BLOCK
    nj = tokens // t
    has_init = s0 is not None
    assert nseq == 1 or nj == 1
    state_spec = pl.BlockSpec((1, nseq, HEADS, HEAD_DIM, HEAD_DIM), lambda b, j: (b, 0, 0, 0, 0))
    conv_spec = pl.BlockSpec((1, nseq, CONV_W - 1, D_MODEL), lambda b, j: (b, 0, 0, 0))
    x_spec = pl.BlockSpec((1, t, D_MODEL), lambda b, j: (b, j, 0))
    in_specs = [x_spec]
    args = [x]
    if has_init:
        in_specs += [state_spec, conv_spec]
        args += [s0, c0]
    for w in weights:
        in_specs.append(_const_spec(w.shape))
        args.append(w)
    kern = functools.partial(_mixer_kernel, t=t, nseq=nseq, has_init=has_init, nj=nj)
    return pl.pallas_call(
        kern,
        grid=(nb, nj),
        in_specs=in_specs,
        out_specs=[x_spec, state_spec, conv_spec],
        out_shape=[
            jax.ShapeDtypeStruct(x.shape, F32),
            jax.ShapeDtypeStruct((nb, nseq, HEADS, HEAD_DIM, HEAD_DIM), F32),
            jax.ShapeDtypeStruct((nb, nseq, CONV_W - 1, D_MODEL), F32),
        ],
        scratch_shapes=[
            pltpu.VMEM((t, D_MODEL), BF16),
            pltpu.VMEM((nseq * HEADS, HEAD_DIM, HEAD_DIM), F32),
            pltpu.VMEM((t, t), jnp.int32),
        ],
        compiler_params=pltpu.CompilerParams(
            dimension_semantics=("arbitrary", "arbitrary"),
            vmem_limit_bytes=VMEM_LIMIT_BYTES),
        name="mixer_init" if has_init else "mixer_zero",
    )(*args)


def _ffn_call(x, p, weights, name):
    nb, tokens, _ = x.shape
    t = TOKEN_BLOCK
    nj = tokens // t
    x_spec = pl.BlockSpec((1, t, D_MODEL), lambda b, j: (b, j, 0))
    p_spec = pl.BlockSpec((1, t, PLE_DIM), lambda b, j: (b, j, 0))
    return pl.pallas_call(
        _ffn_kernel,
        grid=(nb, nj),
        in_specs=[x_spec, p_spec] + [_const_spec(w.shape) for w in weights],
        out_specs=x_spec,
        out_shape=jax.ShapeDtypeStruct(x.shape, F32),
        compiler_params=pltpu.CompilerParams(
            dimension_semantics=("arbitrary", "arbitrary"),
            vmem_limit_bytes=VMEM_LIMIT_BYTES),
        name=name,
    )(x, p, *weights)


def kernel(x_prompt, x_sample, p_prompt, p_sample, state_hgrn, state_conv, lower_bounds, norm_mix, w_in, conv_w, hg_norm, w_branch_a, w_branch_b, w_out, norm_ffn, w_gate_up, w_down, norm_ple, w_ple, w_ple_gate, norm_final):
    depth = w_in.shape[0]
    assert depth == 1
    batch, seq, _ = x_prompt.shape
    dec_batch, dec_seq, _ = x_sample.shape
    assert seq % TOKEN_BLOCK == 0 and TOKEN_BLOCK % dec_seq == 0
    spb = TOKEN_BLOCK // dec_seq
    assert dec_batch % spb == 0
    nsb = dec_batch // spb

    row = lambda a: a.reshape(1, -1)
    mixer_w = [lower_bounds, row(norm_mix[0]), w_in[0].astype(BF16), conv_w[0], row(hg_norm[0]),
               w_branch_a[0].astype(BF16), w_branch_b[0].astype(BF16), w_out[0].astype(BF16)]
    ffn_w = [row(norm_ffn[0]), w_gate_up[0].astype(BF16), w_down[0].astype(BF16), row(norm_ple[0]),
             w_ple[0].astype(BF16), w_ple_gate[0].astype(BF16), row(norm_final)]

    hp, sp, cp = _mixer_call(x_prompt, None, None, mixer_w, nseq=1)
    xs = x_sample.reshape(nsb, TOKEN_BLOCK, D_MODEL)
    s0 = state_hgrn[0].reshape(nsb, spb, HEADS, HEAD_DIM, HEAD_DIM)
    c0 = state_conv[0].reshape(nsb, spb, CONV_W - 1, D_MODEL)
    hs, ss, cs = _mixer_call(xs, s0, c0, mixer_w, nseq=spb)

    y_prompt = _ffn_call(hp, p_prompt[0], ffn_w, "ffn_prompt")
    y_sample = _ffn_call(hs, p_sample[0].reshape(nsb, TOKEN_BLOCK, PLE_DIM), ffn_w, "ffn_sample")

    return (y_prompt,
            y_sample.reshape(dec_batch, dec_seq, D_MODEL),
            sp.reshape(1, batch, HEADS, HEAD_DIM, HEAD_DIM),
            cp.reshape(1, batch, CONV_W - 1, D_MODEL),
            ss.reshape(1, dec_batch, HEADS, HEAD_DIM, HEAD_DIM),
            cs.reshape(1, dec_batch, CONV_W - 1, D_MODEL))
```

```python
import functools

import jax
import jax.numpy as jnp
from jax import lax
from jax.experimental import pallas as pl
from jax.experimental.pallas import tpu as pltpu

D_MODEL = 1024
HEADS = 8
HEAD_DIM = 128
CONV_W = 3
D_FF = 2816
PLE_DIM = 256
EPS = 1e-6
N_SECTIONS = 9
SIDE_SEC0 = 4
SUBLANES = 8
TOKEN_BLOCK = 256
HALF = TOKEN_BLOCK // 2
VMEM_LIMIT_BYTES = 56 * 1024 * 1024

F32 = jnp.float32
BF16 = jnp.bfloat16


def _dot(a, b):
    return jnp.dot(a, b, preferred_element_type=F32)


def _dot_nt(a, b):
    return lax.dot_general(a, b, (((1,), (1,)), ((), ())), preferred_element_type=F32)


def _dot_tn(a, b):
    return lax.dot_general(a, b, (((0,), (0,)), ((), ())), preferred_element_type=F32)


def _rmsnorm(x, g):
    ms = jnp.mean(x * x, axis=-1, keepdims=True)
    return x * lax.rsqrt(ms + EPS) * g


def _sigmoid(x):
    return 1.0 / (1.0 + jnp.exp(-x))


def _level_step(c, blk):
    t, lanes = c.shape
    if blk >= SUBLANES:
        g = t // (2 * blk)
        c4 = c.reshape(g, 2, blk, lanes)
        ev = c4[:, 0]
        od = c4[:, 1]
        r = ev[:, blk - 1:blk, :]
        a = jnp.stack([r - ev, od], axis=1).reshape(t, lanes)
        cn = jnp.stack([ev, od + r], axis=1).reshape(t, lanes)
        return a, cn
    c3 = c.reshape(t // SUBLANES, SUBLANES, lanes)
    sub = lax.broadcasted_iota(jnp.int32, c3.shape, 1)
    r = None
    for grp in range(SUBLANES // (2 * blk)):
        row = grp * 2 * blk + blk - 1
        rg = jnp.broadcast_to(c3[:, row:row + 1, :], c3.shape)
        r = rg if r is None else jnp.where(sub >= grp * 2 * blk, rg, r)
    odd = (sub & blk) != 0
    a = jnp.where(odd, c3, r - c3).reshape(t, lanes)
    cn = jnp.where(odd, c3 + r, c3).reshape(t, lanes)
    return a, cn


def _level_codes(t, seg):
    row = lax.broadcasted_iota(jnp.int32, (t, t), 0)
    col = lax.broadcasted_iota(jnp.int32, (t, t), 1)
    x = row ^ col
    cnt = jnp.zeros((t, t), jnp.int32)
    blk = 1
    while blk < seg:
        cnt = cnt + jnp.where(x >= blk, 1, 0)
        blk *= 2
    lower = jnp.where(row > col, jnp.where(x < seg, cnt, -1), -1)
    return jnp.where(row == col, 0, lower)


def _mixer_kernel(*refs, t, nseq, has_init, nj):
    if has_init:
        (x_ref, s0_ref, c0_ref, lbw_ref, nmix_ref, win_ref, convw_ref, hgn_ref, wa_ref, wb_ref,
         wout_ref, y_ref, sout_ref, cout_ref,
         n_scr, z_scr, st_scr, code_scr) = refs
    else:
        (x_ref, lbw_ref, nmix_ref, win_ref, convw_ref, hgn_ref, wa_ref, wb_ref,
         wout_ref, y_ref, sout_ref, cout_ref,
         n_scr, z_scr, st_scr, code_scr) = refs
        s0_ref = c0_ref = None
    seg = t // nseq
    b = pl.program_id(0)
    j = pl.program_id(1)

    @pl.when((b == 0) & (j == 0))
    def _():
        code_scr[...] = _level_codes(HALF, seg)

    @pl.when(j == 0)
    def _():
        for sq in range(nseq):
            for h in range(HEADS):
                if has_init:
                    st_scr[sq * HEADS + h] = s0_ref[0, sq, h].T
                else:
                    st_scr[sq * HEADS + h] = jnp.zeros((HEAD_DIM, HEAD_DIM), F32)
        if has_init:
            cout_ref[...] = c0_ref[...]
        else:
            cout_ref[...] = jnp.zeros(cout_ref.shape, F32)

    x = x_ref[0]
    n_scr[...] = _rmsnorm(x, nmix_ref[...]).astype(BF16)

    lbw = lbw_ref[...]
    mx = lbw[0:1]
    for r in range(1, lbw.shape[0]):
        mx = jnp.maximum(mx, lbw[r:r + 1])
    den = jnp.zeros_like(mx)
    for r in range(lbw.shape[0]):
        den = den + jnp.exp(lbw[r:r + 1] - mx)
    lb = jnp.exp(lbw[0:1] - mx) / den

    hgn = hgn_ref[...]
    code = code_scr[...]
    pair = 2 * HEAD_DIM

    def proj_cols(sec, c0):
        base = sec * D_MODEL + c0
        return _dot(n_scr[...], win_ref[:, base:base + pair])

    halves = [slice(hf * HALF, (hf + 1) * HALF) for hf in range(t // HALF)]

    n_pairs = HEADS // 2
    head_z = {}

    def head_proj(sec, pr):
        def run():
            head_z[sec, pr] = proj_cols(sec, pr * pair)
        return run

    def side_proj(sec, cb):
        def run():
            z_scr[sec - SIDE_SEC0, :, cb * pair:(cb + 1) * pair] = proj_cols(sec, cb * pair)
        return run

    fill_queue = []

    def fill():
        if fill_queue:
            fill_queue.pop(0)()

    def head(h, q, k, lf, v, sg):
        qb = q.astype(BF16)
        kb = k.astype(BF16)
        p = [jnp.where(code == 0, _dot_nt(qb[rows], kb[rows]), 0.0) for rows in halves]
        p_low = None
        c = lf
        blk, lvl = 1, 1
        while blk < seg:
            a, c = _level_step(c, blk)
            xs = jnp.exp(a)
            if blk == HALF:
                p_low = _dot_nt((q[halves[1]] * xs[halves[1]]).astype(BF16),
                                (k[halves[0]] * xs[halves[0]]).astype(BF16))
            elif blk >= SUBLANES:
                g = HALF // (2 * blk)
                odd = lambda z: z.reshape(g, 2, blk, HEAD_DIM)[:, 1]
                kx = (k * xs).astype(BF16)
                for hf, rows in enumerate(halves):
                    qx = (odd(q[rows]) * odd(xs[rows])).reshape(g * blk, HEAD_DIM).astype(BF16)
                    s = _dot_nt(qx, kx[rows]).reshape(g, blk, HALF)
                    p4 = p[hf].reshape(g, 2, blk, HALF)
                    merged = jnp.where(odd(code) == lvl, s, p4[:, 1])
                    p[hf] = jnp.stack([p4[:, 0], merged], axis=1).reshape(HALF, HALF)
            else:
                qx = (q * xs).astype(BF16)
                kx = (k * xs).astype(BF16)
                for hf, rows in enumerate(halves):
                    p[hf] = jnp.where(code == lvl, _dot_nt(qx[rows], kx[rows]), p[hf])
            if lvl % 2 == 1:
                fill()
            blk *= 2
            lvl += 1
        o_top = _dot(p[0].astype(BF16), v[halves[0]])
        if p_low is None:
            o_bot = _dot(p[1].astype(BF16), v[halves[1]])
        else:
            o_bot = _dot(jnp.concatenate([p_low, p[1]], axis=1).astype(BF16), v)
        o = jnp.concatenate([o_top, o_bot], axis=0)
        inter = []
        for sq in range(nseq):
            rows = slice(sq * seg, (sq + 1) * seg)
            cs = c[rows]
            tot = cs[seg - 1:seg]
            st = st_scr[sq * HEADS + h]
            qd = (q[rows] * jnp.exp(cs)).astype(BF16)
            inter.append(_dot_nt(qd, st.astype(BF16)))
            kd = (k[rows] * jnp.exp(tot - cs)).astype(BF16)
            st_scr[sq * HEADS + h] = st * jnp.exp(tot) + _dot_tn(v[rows], kd)
        o = o + (inter[0] if nseq == 1 else jnp.concatenate(inter, axis=0))
        return (_rmsnorm(o, hgn) * sg).astype(BF16)

    oa_parts = []
    for sec in range(SIDE_SEC0):
        head_proj(sec, 0)()
    for pr in range(n_pairs):
        c0 = pr * pair
        if pr + 1 < n_pairs:
            fill_queue += [head_proj(sec, pr + 1) for sec in range(SIDE_SEC0)]
        fill_queue += [side_proj(sec, pr) for sec in range(SIDE_SEC0, N_SECTIONS)]
        zq = head_z.pop((0, pr))
        q2 = zq * _sigmoid(zq) * (HEAD_DIM ** -0.5)
        sgf = _sigmoid(head_z.pop((1, pr)))
        lbp = lb[:, c0:c0 + pair]
        lf2 = jnp.log(lbp + (1.0 - lbp) * sgf)
        k2 = (1.0 - lbp) * (1.0 - sgf)
        v2 = head_z.pop((2, pr)).astype(BF16)
        zg = head_z.pop((3, pr))
        sg2 = zg * _sigmoid(zg)
        for hh in range(2):
            cols = slice(hh * HEAD_DIM, (hh + 1) * HEAD_DIM)
            oa_parts.append(head(2 * pr + hh, q2[:, cols], k2[:, cols], lf2[:, cols], v2[:, cols],
                                 sg2[:, cols]))
        while fill_queue:
            fill()

    oa = jnp.concatenate(oa_parts, axis=1)
    ya = _dot(oa, wa_ref[...])

    bg = z_scr[0]
    u = z_scr[1] * z_scr[2]
    cw = convw_ref[...]
    rowi = lax.broadcasted_iota(jnp.int32, (seg, D_MODEL), 0)
    conv_parts = []
    for sq in range(nseq):
        us = u[sq * seg:(sq + 1) * seg]
        prev = cout_ref[0, sq]
        s1 = jnp.where(rowi == 0, prev[1:2], pltpu.roll(us, 1, 0))
        s2 = jnp.where(rowi == 0, prev[0:1], jnp.where(rowi == 1, prev[1:2], pltpu.roll(us, 2, 0)))
        conv_parts.append(cw[0:1] * s2 + cw[1:2] * s1 + cw[2:3] * us)
        cout_ref[0, sq] = us[seg - (CONV_W - 1):seg]
    conv = conv_parts[0] if nseq == 1 else jnp.concatenate(conv_parts, axis=0)
    ob = (bg * conv).astype(BF16)

    mix = _sigmoid(z_scr[3]) * ya + _sigmoid(z_scr[4]) * _dot(ob, wb_ref[...])
    y_ref[0] = x + _dot(mix.astype(BF16), wout_ref[...])

    @pl.when(j == nj - 1)
    def _():
        for sq in range(nseq):
            for h in range(HEADS):
                sout_ref[0, sq, h] = st_scr[sq * HEADS + h].T


def _ffn_kernel(x_ref, p_ref, nffn_ref, wgu_ref, wdown_ref, nple_ref, wple_ref, wpg_ref, nfin_ref,
                y_ref):
    x = x_ref[0]
    n = _rmsnorm(x, nffn_ref[...]).astype(BF16)
    gate = _dot(n, wgu_ref[:, 0:D_FF])
    up = _dot(n, wgu_ref[:, D_FF:2 * D_FF])
    hidden = (gate * _sigmoid(gate) * up).astype(BF16)
    x = x + _dot(hidden, wdown_ref[...])
    n2 = _rmsnorm(x, nple_ref[...]).astype(BF16)
    ple = _dot(p_ref[0].astype(BF16), wple_ref[...])
    x = x + _sigmoid(_dot(n2, wpg_ref[...])) * ple
    y_ref[0] = _rmsnorm(x, nfin_ref[...])


def _const_spec(shape):
    zeros = (0,) * len(shape)
    return pl.BlockSpec(shape, lambda b, j: zeros, pipeline_mode=pl.Buffered(1))


def _mixer_call(x, s0, c0, weights, *, nseq):
    nb, tokens, _ = x.shape
    t = TOKEN_BLOCK
    nj = tokens // t
    has_init = s0 is not None
    assert nseq == 1 or nj == 1
    state_spec = pl.BlockSpec((1, nseq, HEADS, HEAD_DIM, HEAD_DIM), lambda b, j: (b, 0, 0, 0, 0))
    conv_spec = pl.BlockSpec((1, nseq, CONV_W - 1, D_MODEL), lambda b, j: (b, 0, 0, 0))
    x_spec = pl.BlockSpec((1, t, D_MODEL), lambda b, j: (b, j, 0))
    in_specs = [x_spec]
    args = [x]
    if has_init:
        in_specs += [state_spec, conv_spec]
        args += [s0, c0]
    for w in weights:
        in_specs.append(_const_spec(w.shape))
        args.append(w)
    kern = functools.partial(_mixer_kernel, t=t, nseq=nseq, has_init=has_init, nj=nj)
    return pl.pallas_call(
        kern,
        grid=(nb, nj),
        in_specs=in_specs,
        out_specs=[x_spec, state_spec, conv_spec],
        out_shape=[
            jax.ShapeDtypeStruct(x.shape, F32),
            jax.ShapeDtypeStruct((nb, nseq, HEADS, HEAD_DIM, HEAD_DIM), F32),
            jax.ShapeDtypeStruct((nb, nseq, CONV_W - 1, D_MODEL), F32),
        ],
        scratch_shapes=[
            pltpu.VMEM((t, D_MODEL), BF16),
            pltpu.VMEM((N_SECTIONS - SIDE_SEC0, t, D_MODEL), F32),
            pltpu.VMEM((nseq * HEADS, HEAD_DIM, HEAD_DIM), F32),
            pltpu.VMEM((HALF, HALF), jnp.int32),
        ],
        compiler_params=pltpu.CompilerParams(
            dimension_semantics=("arbitrary", "arbitrary"),
            vmem_limit_bytes=VMEM_LIMIT_BYTES),
        name="mixer_init" if has_init else "mixer_zero",
    )(*args)


def _ffn_call(x, p, weights, name):
    nb, tokens, _ = x.shape
    t = TOKEN_BLOCK
    nj = tokens // t
    x_spec = pl.BlockSpec((1, t, D_MODEL), lambda b, j: (b, j, 0))
    p_spec = pl.BlockSpec((1, t, PLE_DIM), lambda b, j: (b, j, 0))
    return pl.pallas_call(
        _ffn_kernel,
        grid=(nb, nj),
        in_specs=[x_spec, p_spec] + [_const_spec(w.shape) for w in weights],
        out_specs=x_spec,
        out_shape=jax.ShapeDtypeStruct(x.shape, F32),
        compiler_params=pltpu.CompilerParams(
            dimension_semantics=("arbitrary", "arbitrary"),
            vmem_limit_bytes=VMEM_LIMIT_BYTES),
        name=name,
    )(x, p, *weights)


def kernel(x_prompt, x_sample, p_prompt, p_sample, state_hgrn, state_conv, lower_bounds, norm_mix, w_in, conv_w, hg_norm, w_branch_a, w_branch_b, w_out, norm_ffn, w_gate_up, w_down, norm_ple, w_ple, w_ple_gate, norm_final):
    depth = w_in.shape[0]
    assert depth == 1
    batch, seq, _ = x_prompt.shape
    dec_batch, dec_seq, _ = x_sample.shape
    assert seq % TOKEN_BLOCK == 0 and TOKEN_BLOCK % dec_seq == 0
    spb = TOKEN_BLOCK // dec_seq
    assert dec_batch % spb == 0
    nsb = dec_batch // spb

    row = lambda a: a.reshape(1, -1)
    mixer_w = [lower_bounds, row(norm_mix[0]), w_in[0].astype(BF16), conv_w[0], row(hg_norm[0]),
               w_branch_a[0].astype(BF16), w_branch_b[0].astype(BF16), w_out[0].astype(BF16)]
    ffn_w = [row(norm_ffn[0]), w_gate_up[0].astype(BF16), w_down[0].astype(BF16), row(norm_ple[0]),
             w_ple[0].astype(BF16), w_ple_gate[0].astype(BF16), row(norm_final)]

    hp, sp, cp = _mixer_call(x_prompt, None, None, mixer_w, nseq=1)
    xs = x_sample.reshape(nsb, TOKEN_BLOCK, D_MODEL)
    s0 = state_hgrn[0].reshape(nsb, spb, HEADS, HEAD_DIM, HEAD_DIM)
    c0 = state_conv[0].reshape(nsb, spb, CONV_W - 1, D_MODEL)
    hs, ss, cs = _mixer_call(xs, s0, c0, mixer_w, nseq=spb)

    y_prompt = _ffn_call(hp, p_prompt[0], ffn_w, "ffn_prompt")
    y_sample = _ffn_call(hs, p_sample[0].reshape(nsb, TOKEN_BLOCK, PLE_DIM), ffn_w, "ffn_sample")

    return (y_prompt,
            y_sample.reshape(dec_batch, dec_seq, D_MODEL),
            sp.reshape(1, batch, HEADS, HEAD_DIM, HEAD_DIM),
            cp.reshape(1, batch, CONV_W - 1, D_MODEL),
            ss.reshape(1, dec_batch, HEADS, HEAD_DIM, HEAD_DIM),
            cs.reshape(1, dec_batch, CONV_W - 1, D_MODEL))
```

```python
import functools

import jax
import jax.numpy as jnp
from jax import lax
from jax.experimental import pallas as pl
from jax.experimental.pallas import tpu as pltpu

D_MODEL = 1024
HEADS = 8
HEAD_DIM = 128
CONV_W = 3
D_FF = 2816
PLE_DIM = 256
EPS = 1e-6
N_SECTIONS = 9
SIDE_SEC0 = 4
SUBLANES = 8
BF16_ROWS = 16
TOKEN_BLOCK = 256
HALF = TOKEN_BLOCK // 2
FFN_BLOCK = 512
VMEM_LIMIT_BYTES = 56 * 1024 * 1024

F32 = jnp.float32
BF16 = jnp.bfloat16


def _dot(a, b):
    return jnp.dot(a, b, preferred_element_type=F32)


def _dot_nt(a, b):
    return lax.dot_general(a, b, (((1,), (1,)), ((), ())), preferred_element_type=F32)


def _dot_tn(a, b):
    return lax.dot_general(a, b, (((0,), (0,)), ((), ())), preferred_element_type=F32)


def _rmsnorm(x, g):
    ms = jnp.mean(x * x, axis=-1, keepdims=True)
    return x * lax.rsqrt(ms + EPS) * g


def _sigmoid(x):
    return 1.0 / (1.0 + jnp.exp(-x))


def _level_step(c, blk):
    t, lanes = c.shape
    if blk >= SUBLANES:
        g = t // (2 * blk)
        c4 = c.reshape(g, 2, blk, lanes)
        ev = c4[:, 0]
        od = c4[:, 1]
        r = ev[:, blk - 1:blk, :]
        a = jnp.stack([r - ev, od], axis=1).reshape(t, lanes)
        cn = jnp.stack([ev, od + r], axis=1).reshape(t, lanes)
        return a, cn
    c3 = c.reshape(t // SUBLANES, SUBLANES, lanes)
    sub = lax.broadcasted_iota(jnp.int32, c3.shape, 1)
    if blk == 1:
        odd = (sub & 1) != 0
        prev = pltpu.roll(c3, 1, 1)
        a = jnp.where(odd, c3, 0.0).reshape(t, lanes)
        cn = jnp.where(odd, c3 + prev, c3).reshape(t, lanes)
        return a, cn
    r = None
    for grp in range(SUBLANES // (2 * blk)):
        row = grp * 2 * blk + blk - 1
        rg = jnp.broadcast_to(c3[:, row:row + 1, :], c3.shape)
        r = rg if r is None else jnp.where(sub >= grp * 2 * blk, rg, r)
    odd = (sub & blk) != 0
    a = jnp.where(odd, c3, r - c3).reshape(t, lanes)
    cn = jnp.where(odd, c3 + r, c3).reshape(t, lanes)
    return a, cn


def _level_codes(t, seg):
    row = lax.broadcasted_iota(jnp.int32, (t, t), 0)
    col = lax.broadcasted_iota(jnp.int32, (t, t), 1)
    x = row ^ col
    cnt = jnp.zeros((t, t), jnp.int32)
    blk = 1
    while blk < seg:
        cnt = cnt + jnp.where(x >= blk, 1, 0)
        blk *= 2
    lower = jnp.where(row > col, jnp.where(x < seg, cnt, -1), -1)
    return jnp.where(row == col, 0, lower)


def _mixer_kernel(*refs, t, nseq, has_init, nj):
    if has_init:
        (x_ref, s0_ref, c0_ref, lbw_ref, nmix_ref, win_ref, convw_ref, hgn_ref, wa_ref, wb_ref,
         wout_ref, y_ref, sout_ref, cout_ref,
         n_scr, z_scr, st_scr, code_scr) = refs
    else:
        (x_ref, lbw_ref, nmix_ref, win_ref, convw_ref, hgn_ref, wa_ref, wb_ref,
         wout_ref, y_ref, sout_ref, cout_ref,
         n_scr, z_scr, st_scr, code_scr) = refs
        s0_ref = c0_ref = None
    seg = t // nseq
    b = pl.program_id(0)
    j = pl.program_id(1)

    @pl.when((b == 0) & (j == 0))
    def _():
        code_scr[...] = _level_codes(HALF, seg)

    @pl.when(j == 0)
    def _():
        for sq in range(nseq):
            for h in range(HEADS):
                if has_init:
                    st_scr[sq * HEADS + h] = s0_ref[0, sq, h].T
                else:
                    st_scr[sq * HEADS + h] = jnp.zeros((HEAD_DIM, HEAD_DIM), F32)
        if has_init:
            cout_ref[...] = c0_ref[...]
        else:
            cout_ref[...] = jnp.zeros(cout_ref.shape, F32)

    x = x_ref[0]
    n_scr[...] = _rmsnorm(x, nmix_ref[...]).astype(BF16)

    lbw = lbw_ref[...]
    mx = lbw[0:1]
    for r in range(1, lbw.shape[0]):
        mx = jnp.maximum(mx, lbw[r:r + 1])
    den = jnp.zeros_like(mx)
    for r in range(lbw.shape[0]):
        den = den + jnp.exp(lbw[r:r + 1] - mx)
    lb = jnp.exp(lbw[0:1] - mx) / den

    hgn = hgn_ref[...]
    code = code_scr[...]
    pair = 2 * HEAD_DIM

    def proj_cols(sec, c0):
        base = sec * D_MODEL + c0
        return _dot(n_scr[...], win_ref[:, base:base + pair])

    halves = [slice(hf * HALF, (hf + 1) * HALF) for hf in range(t // HALF)]

    n_pairs = HEADS // 2
    head_z = {}

    def head_proj(sec, pr):
        def run():
            head_z[sec, pr] = proj_cols(sec, pr * pair)
        return run

    def side_proj(sec, cb):
        def run():
            z_scr[sec - SIDE_SEC0, :, cb * pair:(cb + 1) * pair] = proj_cols(sec, cb * pair)
        return run

    fill_queue = []

    def fill():
        if fill_queue:
            fill_queue.pop(0)()

    def head(h, q, k, lf, v, sg):
        qb = q.astype(BF16)
        kb = k.astype(BF16)
        p = [jnp.where(code == 0, _dot_nt(qb[rows], kb[rows]), 0.0) for rows in halves]
        p_low = None
        c = lf
        blk, lvl = 1, 1
        while blk < seg:
            a, c = _level_step(c, blk)
            xs = jnp.exp2(a)
            xb = xs.astype(BF16)
            if blk == HALF:
                p_low = _dot_nt(qb[halves[1]] * xb[halves[1]], kb[halves[0]] * xb[halves[0]])
            elif blk >= SUBLANES:
                g = HALF // (2 * blk)
                odd = lambda z: z.reshape(g, 2, blk, HEAD_DIM)[:, 1].reshape(g * blk, HEAD_DIM)
                kx = kb * xb
                for hf, rows in enumerate(halves):
                    if blk >= BF16_ROWS:
                        qx = odd(qb[rows]) * odd(xb[rows])
                    else:
                        qx = (odd(q[rows]) * odd(xs[rows])).astype(BF16)
                    s = _dot_nt(qx, kx[rows]).reshape(g, blk, HALF)
                    p4 = p[hf].reshape(g, 2, blk, HALF)
                    lvl_mask = code.reshape(g, 2, blk, HALF)[:, 1] == lvl
                    merged = jnp.where(lvl_mask, s, p4[:, 1])
                    p[hf] = jnp.stack([p4[:, 0], merged], axis=1).reshape(HALF, HALF)
            else:
                qx = qb * xb
                kx = kb * xb
                for hf, rows in enumerate(halves):
                    p[hf] = jnp.where(code == lvl, _dot_nt(qx[rows], kx[rows]), p[hf])
            if lvl % 2 == 1:
                fill()
            blk *= 2
            lvl += 1
        o_top = _dot(p[0].astype(BF16), v[halves[0]])
        if p_low is None:
            o_bot = _dot(p[1].astype(BF16), v[halves[1]])
        else:
            o_bot = _dot(jnp.concatenate([p_low, p[1]], axis=1).astype(BF16), v)
        o = jnp.concatenate([o_top, o_bot], axis=0)
        inter = []
        for sq in range(nseq):
            rows = slice(sq * seg, (sq + 1) * seg)
            cs = c[rows]
            tot = cs[seg - 1:seg]
            st = st_scr[sq * HEADS + h]
            qd = (q[rows] * jnp.exp2(cs)).astype(BF16)
            inter.append(_dot_nt(qd, st.astype(BF16)))
            kd = (k[rows] * jnp.exp2(tot - cs)).astype(BF16)
            st_scr[sq * HEADS + h] = st * jnp.exp2(tot) + _dot_tn(v[rows], kd)
        o = o + (inter[0] if nseq == 1 else jnp.concatenate(inter, axis=0))
        ms = jnp.mean(o * o, axis=-1, keepdims=True)
        return (o * lax.rsqrt(ms + EPS * HEAD_DIM) * hgn * sg).astype(BF16)

    oa_parts = []
    for sec in range(SIDE_SEC0):
        head_proj(sec, 0)()
    for pr in range(n_pairs):
        c0 = pr * pair
        if pr + 1 < n_pairs:
            fill_queue += [head_proj(sec, pr + 1) for sec in range(SIDE_SEC0)]
        fill_queue += [side_proj(sec, pr) for sec in range(SIDE_SEC0, N_SECTIONS)]
        zq = head_z.pop((0, pr))
        q2 = zq * _sigmoid(zq)
        sgf = _sigmoid(head_z.pop((1, pr)))
        lbp = lb[:, c0:c0 + pair]
        lf2 = jnp.log2(lbp + (1.0 - lbp) * sgf)
        k2 = (1.0 - lbp) * (1.0 - sgf)
        v2 = head_z.pop((2, pr)).astype(BF16)
        zg = head_z.pop((3, pr))
        sg2 = zg * _sigmoid(zg)
        for hh in range(2):
            cols = slice(hh * HEAD_DIM, (hh + 1) * HEAD_DIM)
            oa_parts.append(head(2 * pr + hh, q2[:, cols], k2[:, cols], lf2[:, cols], v2[:, cols],
                                 sg2[:, cols]))
        while fill_queue:
            fill()

    oa = jnp.concatenate(oa_parts, axis=1)
    ya = _dot(oa, wa_ref[...])

    bg = z_scr[0]
    u = z_scr[1] * z_scr[2]
    cw = convw_ref[...]
    rowi = lax.broadcasted_iota(jnp.int32, (seg, D_MODEL), 0)
    conv_parts = []
    for sq in range(nseq):
        us = u[sq * seg:(sq + 1) * seg]
        prev = cout_ref[0, sq]
        s1 = jnp.where(rowi == 0, prev[1:2], pltpu.roll(us, 1, 0))
        s2 = jnp.where(rowi == 0, prev[0:1], jnp.where(rowi == 1, prev[1:2], pltpu.roll(us, 2, 0)))
        conv_parts.append(cw[0:1] * s2 + cw[1:2] * s1 + cw[2:3] * us)
        cout_ref[0, sq] = us[seg - (CONV_W - 1):seg]
    conv = conv_parts[0] if nseq == 1 else jnp.concatenate(conv_parts, axis=0)
    ob = (bg * conv).astype(BF16)

    mix = _sigmoid(z_scr[3]) * ya + _sigmoid(z_scr[4]) * _dot(ob, wb_ref[...])
    y_ref[0] = x + _dot(mix.astype(BF16), wout_ref[...])

    @pl.when(j == nj - 1)
    def _():
        for sq in range(nseq):
            for h in range(HEADS):
                sout_ref[0, sq, h] = st_scr[sq * HEADS + h].T


def _ffn_kernel(x_ref, p_ref, nffn_ref, wgu_ref, wdown_ref, nple_ref, wple_ref, wpg_ref, nfin_ref,
                y_ref):
    x = x_ref[...]
    n = _rmsnorm(x, nffn_ref[...]).astype(BF16)
    gate = _dot(n, wgu_ref[:, 0:D_FF])
    up = _dot(n, wgu_ref[:, D_FF:2 * D_FF])
    hidden = (gate * _sigmoid(gate) * up).astype(BF16)
    x = x + _dot(hidden, wdown_ref[...])
    n2 = _rmsnorm(x, nple_ref[...]).astype(BF16)
    ple = _dot(p_ref[...].astype(BF16), wple_ref[...])
    x = x + _sigmoid(_dot(n2, wpg_ref[...])) * ple
    y_ref[...] = _rmsnorm(x, nfin_ref[...])


def _const_spec(shape, grid_rank=2):
    zeros = (0,) * len(shape)
    index_map = (lambda i: zeros) if grid_rank == 1 else (lambda b, j: zeros)
    return pl.BlockSpec(shape, index_map, pipeline_mode=pl.Buffered(1))


def _mixer_call(x, s0, c0, weights, *, nseq):
    nb, tokens, _ = x.shape
    t = TOKEN_BLOCK
    nj = tokens // t
    has_init = s0 is not None
    assert nseq == 1 or nj == 1
    state_spec = pl.BlockSpec((1, nseq, HEADS, HEAD_DIM, HEAD_DIM), lambda b, j: (b, 0, 0, 0, 0))
    conv_spec = pl.BlockSpec((1, nseq, CONV_W - 1, D_MODEL), lambda b, j: (b, 0, 0, 0))
    x_spec = pl.BlockSpec((1, t, D_MODEL), lambda b, j: (b, j, 0))
    in_specs = [x_spec]
    args = [x]
    if has_init:
        in_specs += [state_spec, conv_spec]
        args += [s0, c0]
    for w in weights:
        in_specs.append(_const_spec(w.shape))
        args.append(w)
    kern = functools.partial(_mixer_kernel, t=t, nseq=nseq, has_init=has_init, nj=nj)
    return pl.pallas_call(
        kern,
        grid=(nb, nj),
        in_specs=in_specs,
        out_specs=[x_spec, state_spec, conv_spec],
        out_shape=[
            jax.ShapeDtypeStruct(x.shape, F32),
            jax.ShapeDtypeStruct((nb, nseq, HEADS, HEAD_DIM, HEAD_DIM), F32),
            jax.ShapeDtypeStruct((nb, nseq, CONV_W - 1, D_MODEL), F32),
        ],
        scratch_shapes=[
            pltpu.VMEM((t, D_MODEL), BF16),
            pltpu.VMEM((N_SECTIONS - SIDE_SEC0, t, D_MODEL), F32),
            pltpu.VMEM((nseq * HEADS, HEAD_DIM, HEAD_DIM), F32),
            pltpu.VMEM((HALF, HALF), jnp.int32),
        ],
        compiler_params=pltpu.CompilerParams(
            dimension_semantics=("arbitrary", "arbitrary"),
            vmem_limit_bytes=VMEM_LIMIT_BYTES),
        name="mixer_init" if has_init else "mixer_zero",
    )(*args)


def _ffn_call(x, p, weights, name):
    tokens = x.shape[0]
    t = FFN_BLOCK
    assert tokens % t == 0
    x_spec = pl.BlockSpec((t, D_MODEL), lambda i: (i, 0))
    p_spec = pl.BlockSpec((t, PLE_DIM), lambda i: (i, 0))
    return pl.pallas_call(
        _ffn_kernel,
        grid=(tokens // t,),
        in_specs=[x_spec, p_spec] + [_const_spec(w.shape, 1) for w in weights],
        out_specs=x_spec,
        out_shape=jax.ShapeDtypeStruct(x.shape, F32),
        compiler_params=pltpu.CompilerParams(
            dimension_semantics=("arbitrary",),
            vmem_limit_bytes=VMEM_LIMIT_BYTES),
        name=name,
    )(x, p, *weights)


def kernel(x_prompt, x_sample, p_prompt, p_sample, state_hgrn, state_conv, lower_bounds, norm_mix, w_in, conv_w, hg_norm, w_branch_a, w_branch_b, w_out, norm_ffn, w_gate_up, w_down, norm_ple, w_ple, w_ple_gate, norm_final):
    depth = w_in.shape[0]
    assert depth == 1
    batch, seq, _ = x_prompt.shape
    dec_batch, dec_seq, _ = x_sample.shape
    assert seq % TOKEN_BLOCK == 0 and TOKEN_BLOCK % dec_seq == 0
    spb = TOKEN_BLOCK // dec_seq
    assert dec_batch % spb == 0
    nsb = dec_batch // spb

    row = lambda a: a.reshape(1, -1)
    mixer_w = [lower_bounds, row(norm_mix[0]), w_in[0].astype(BF16), conv_w[0], row(hg_norm[0]),
               w_branch_a[0].astype(BF16), w_branch_b[0].astype(BF16), w_out[0].astype(BF16)]
    ffn_w = [row(norm_ffn[0]), w_gate_up[0].astype(BF16), w_down[0].astype(BF16), row(norm_ple[0]),
             w_ple[0].astype(BF16), w_ple_gate[0].astype(BF16), row(norm_final)]

    hp, sp, cp = _mixer_call(x_prompt, None, None, mixer_w, nseq=1)
    xs = x_sample.reshape(nsb, TOKEN_BLOCK, D_MODEL)
    s0 = state_hgrn[0].reshape(nsb, spb, HEADS, HEAD_DIM, HEAD_DIM)
    c0 = state_conv[0].reshape(nsb, spb, CONV_W - 1, D_MODEL)
    hs, ss, cs = _mixer_call(xs, s0, c0, mixer_w, nseq=spb)

    y_prompt = _ffn_call(hp.reshape(-1, D_MODEL), p_prompt[0].reshape(-1, PLE_DIM), ffn_w,
                         "ffn_prompt")
    y_sample = _ffn_call(hs.reshape(-1, D_MODEL), p_sample[0].reshape(-1, PLE_DIM), ffn_w,
                         "ffn_sample")

    return (y_prompt.reshape(batch, seq, D_MODEL),
            y_sample.reshape(dec_batch, dec_seq, D_MODEL),
            sp.reshape(1, batch, HEADS, HEAD_DIM, HEAD_DIM),
            cp.reshape(1, batch, CONV_W - 1, D_MODEL),
            ss.reshape(1, dec_batch, HEADS, HEAD_DIM, HEAD_DIM),
            cs.reshape(1, dec_batch, CONV_W - 1, D_MODEL))
```

```python
import functools

import jax
import jax.numpy as jnp
from jax import lax
from jax.experimental import pallas as pl
from jax.experimental.pallas import tpu as pltpu

D_MODEL = 1024
HEADS = 8
HEAD_DIM = 128
CONV_W = 3
D_FF = 2816
PLE_DIM = 256
EPS = 1e-6
N_SECTIONS = 9
SIDE_SEC0 = 4
SUBLANES = 8
BF16_ROWS = 16
TOKEN_BLOCK = 256
HALF = TOKEN_BLOCK // 2
FFN_BLOCK = 512
VMEM_LIMIT_BYTES = 56 * 1024 * 1024

F32 = jnp.float32
BF16 = jnp.bfloat16


def _dot(a, b):
    return jnp.dot(a, b, preferred_element_type=F32)


def _dot_nt(a, b):
    return lax.dot_general(a, b, (((1,), (1,)), ((), ())), preferred_element_type=F32)


def _dot_tn(a, b):
    return lax.dot_general(a, b, (((0,), (0,)), ((), ())), preferred_element_type=F32)


def _rmsnorm(x, g):
    ms = jnp.mean(x * x, axis=-1, keepdims=True)
    return x * lax.rsqrt(ms + EPS) * g


def _sigmoid(x):
    return 1.0 / (1.0 + jnp.exp(-x))


def _level_step(c, blk):
    t, lanes = c.shape
    if blk >= SUBLANES:
        g = t // (2 * blk)
        c4 = c.reshape(g, 2, blk, lanes)
        ev = c4[:, 0]
        od = c4[:, 1]
        r = ev[:, blk - 1:blk, :]
        a = jnp.stack([r - ev, od], axis=1).reshape(t, lanes)
        cn = jnp.stack([ev, od + r], axis=1).reshape(t, lanes)
        return a, cn
    c3 = c.reshape(t // SUBLANES, SUBLANES, lanes)
    sub = lax.broadcasted_iota(jnp.int32, c3.shape, 1)
    if blk == 1:
        odd = (sub & 1) != 0
        prev = pltpu.roll(c3, 1, 1)
        a = jnp.where(odd, c3, 0.0).reshape(t, lanes)
        cn = jnp.where(odd, c3 + prev, c3).reshape(t, lanes)
        return a, cn
    r = None
    for grp in range(SUBLANES // (2 * blk)):
        row = grp * 2 * blk + blk - 1
        rg = jnp.broadcast_to(c3[:, row:row + 1, :], c3.shape)
        r = rg if r is None else jnp.where(sub >= grp * 2 * blk, rg, r)
    odd = (sub & blk) != 0
    a = jnp.where(odd, c3, r - c3).reshape(t, lanes)
    cn = jnp.where(odd, c3 + r, c3).reshape(t, lanes)
    return a, cn


def _level_codes(t, seg):
    row = lax.broadcasted_iota(jnp.int32, (t, t), 0)
    col = lax.broadcasted_iota(jnp.int32, (t, t), 1)
    x = row ^ col
    cnt = jnp.zeros((t, t), jnp.int32)
    blk = 1
    while blk < seg:
        cnt = cnt + jnp.where(x >= blk, 1, 0)
        blk *= 2
    lower = jnp.where(row > col, jnp.where(x < seg, cnt, -1), -1)
    return jnp.where(row == col, 0, lower)


def _mixer_kernel(*refs, t, nseq, has_init, nj):
    if has_init:
        (x_ref, s0_ref, c0_ref, lbw_ref, nmix_ref, win_ref, convw_ref, hgn_ref, wa_ref, wb_ref,
         wout_ref, y_ref, sout_ref, cout_ref,
         n_scr, z_scr, st_scr, code_scr) = refs
    else:
        (x_ref, lbw_ref, nmix_ref, win_ref, convw_ref, hgn_ref, wa_ref, wb_ref,
         wout_ref, y_ref, sout_ref, cout_ref,
         n_scr, z_scr, st_scr, code_scr) = refs
        s0_ref = c0_ref = None
    seg = t // nseq
    b = pl.program_id(0)
    j = pl.program_id(1)

    @pl.when((b == 0) & (j == 0))
    def _():
        code_scr[...] = _level_codes(HALF, seg)

    @pl.when(j == 0)
    def _():
        for sq in range(nseq):
            for h in range(HEADS):
                if has_init:
                    st_scr[sq * HEADS + h] = s0_ref[0, sq, h].T
                else:
                    st_scr[sq * HEADS + h] = jnp.zeros((HEAD_DIM, HEAD_DIM), F32)
        if has_init:
            cout_ref[...] = c0_ref[...]
        else:
            cout_ref[...] = jnp.zeros(cout_ref.shape, F32)

    x = x_ref[0]
    n_scr[...] = _rmsnorm(x, nmix_ref[...]).astype(BF16)

    lbw = lbw_ref[...]
    mx = lbw[0:1]
    for r in range(1, lbw.shape[0]):
        mx = jnp.maximum(mx, lbw[r:r + 1])
    den = jnp.zeros_like(mx)
    for r in range(lbw.shape[0]):
        den = den + jnp.exp(lbw[r:r + 1] - mx)
    lb = jnp.exp(lbw[0:1] - mx) / den

    hgn = hgn_ref[...]
    code = code_scr[...]
    pair = 2 * HEAD_DIM

    def proj_cols(sec, c0):
        base = sec * D_MODEL + c0
        return _dot(n_scr[...], win_ref[:, base:base + pair])

    halves = [slice(hf * HALF, (hf + 1) * HALF) for hf in range(t // HALF)]

    n_pairs = HEADS // 2
    head_z = {}

    def head_proj(sec, pr):
        def run():
            head_z[sec, pr] = proj_cols(sec, pr * pair)
        return run

    def side_proj(sec, cb):
        def run():
            z_scr[sec - SIDE_SEC0, :, cb * pair:(cb + 1) * pair] = proj_cols(sec, cb * pair)
        return run

    fill_queue = []

    def fill():
        if fill_queue:
            fill_queue.pop(0)()

    def head(h, q, k, lf, v, sg):
        qb = q.astype(BF16)
        kb = k.astype(BF16)
        p = [jnp.where(code == 0, _dot_nt(qb[rows], kb[rows]), 0.0) for rows in halves]
        p_low = None
        c = lf
        blk, lvl = 1, 1
        while blk < seg:
            a, c = _level_step(c, blk)
            xs = jnp.exp2(a)
            xb = xs.astype(BF16)
            if blk == HALF:
                p_low = _dot_nt(qb[halves[1]] * xb[halves[1]], kb[halves[0]] * xb[halves[0]])
            elif blk >= SUBLANES:
                g = HALF // (2 * blk)
                odd = lambda z: z.reshape(g, 2, blk, HEAD_DIM)[:, 1].reshape(g * blk, HEAD_DIM)
                kx = kb * xb
                for hf, rows in enumerate(halves):
                    if blk >= BF16_ROWS:
                        qx = odd(qb[rows]) * odd(xb[rows])
                    else:
                        qx = (odd(q[rows]) * odd(xs[rows])).astype(BF16)
                    s = _dot_nt(qx, kx[rows]).reshape(g, blk, HALF)
                    p4 = p[hf].reshape(g, 2, blk, HALF)
                    lvl_mask = code.reshape(g, 2, blk, HALF)[:, 1] == lvl
                    merged = jnp.where(lvl_mask, s, p4[:, 1])
                    p[hf] = jnp.stack([p4[:, 0], merged], axis=1).reshape(HALF, HALF)
            else:
                qx = qb * xb
                kx = kb * xb
                for hf, rows in enumerate(halves):
                    p[hf] = jnp.where(code == lvl, _dot_nt(qx[rows], kx[rows]), p[hf])
            blk *= 2
            lvl += 1
        o_top = _dot(p[0].astype(BF16), v[halves[0]])
        if p_low is None:
            o_bot = _dot(p[1].astype(BF16), v[halves[1]])
        else:
            o_bot = _dot(jnp.concatenate([p_low, p[1]], axis=1).astype(BF16), v)
        o = jnp.concatenate([o_top, o_bot], axis=0)
        inter = []
        for sq in range(nseq):
            rows = slice(sq * seg, (sq + 1) * seg)
            cs = c[rows]
            tot = cs[seg - 1:seg]
            st = st_scr[sq * HEADS + h]
            qd = (q[rows] * jnp.exp2(cs)).astype(BF16)
            inter.append(_dot_nt(qd, st.astype(BF16)))
            kd = (k[rows] * jnp.exp2(tot - cs)).astype(BF16)
            st_scr[sq * HEADS + h] = st * jnp.exp2(tot) + _dot_tn(v[rows], kd)
        o = o + (inter[0] if nseq == 1 else jnp.concatenate(inter, axis=0))
        ms = jnp.mean(o * o, axis=-1, keepdims=True)
        return (o * lax.rsqrt(ms + EPS * HEAD_DIM) * hgn * sg).astype(BF16)

    oa_parts = []
    for sec in range(SIDE_SEC0):
        head_proj(sec, 0)()
    for pr in range(n_pairs):
        c0 = pr * pair
        if pr + 1 < n_pairs:
            fill_queue += [head_proj(sec, pr + 1) for sec in range(SIDE_SEC0)]
        fill_queue += [side_proj(sec, pr) for sec in range(SIDE_SEC0, N_SECTIONS)]
        zq = head_z.pop((0, pr))
        q2 = zq * _sigmoid(zq)
        sgf = _sigmoid(head_z.pop((1, pr)))
        lbp = lb[:, c0:c0 + pair]
        lf2 = jnp.log2(lbp + (1.0 - lbp) * sgf)
        k2 = (1.0 - lbp) * (1.0 - sgf)
        v2 = head_z.pop((2, pr)).astype(BF16)
        zg = head_z.pop((3, pr))
        sg2 = zg * _sigmoid(zg)
        for hh in range(2):
            cols = slice(hh * HEAD_DIM, (hh + 1) * HEAD_DIM)
            oa_parts.append(head(2 * pr + hh, q2[:, cols], k2[:, cols], lf2[:, cols], v2[:, cols],
                                 sg2[:, cols]))
        while fill_queue:
            fill()

    oa = jnp.concatenate(oa_parts, axis=1)
    ya = _dot(oa, wa_ref[...])

    bg = z_scr[0]
    u = z_scr[1] * z_scr[2]
    cw = convw_ref[...]
    rowi = lax.broadcasted_iota(jnp.int32, (seg, D_MODEL), 0)
    conv_parts = []
    for sq in range(nseq):
        us = u[sq * seg:(sq + 1) * seg]
        prev = cout_ref[0, sq]
        s1 = jnp.where(rowi == 0, prev[1:2], pltpu.roll(us, 1, 0))
        s2 = jnp.where(rowi == 0, prev[0:1], jnp.where(rowi == 1, prev[1:2], pltpu.roll(us, 2, 0)))
        conv_parts.append(cw[0:1] * s2 + cw[1:2] * s1 + cw[2:3] * us)
        cout_ref[0, sq] = us[seg - (CONV_W - 1):seg]
    conv = conv_parts[0] if nseq == 1 else jnp.concatenate(conv_parts, axis=0)
    ob = (bg * conv).astype(BF16)

    mix = _sigmoid(z_scr[3]) * ya + _sigmoid(z_scr[4]) * _dot(ob, wb_ref[...])
    y_ref[0] = x + _dot(mix.astype(BF16), wout_ref[...])

    @pl.when(j == nj - 1)
    def _():
        for sq in range(nseq):
            for h in range(HEADS):
                sout_ref[0, sq, h] = st_scr[sq * HEADS + h].T


def _ffn_kernel(x_ref, p_ref, nffn_ref, wgu_ref, wdown_ref, nple_ref, wple_ref, wpg_ref, nfin_ref,
                y_ref):
    x = x_ref[...]
    n = _rmsnorm(x, nffn_ref[...]).astype(BF16)
    gate = _dot(n, wgu_ref[:, 0:D_FF])
    up = _dot(n, wgu_ref[:, D_FF:2 * D_FF])
    hidden = (gate * _sigmoid(gate) * up).astype(BF16)
    x = x + _dot(hidden, wdown_ref[...])
    n2 = _rmsnorm(x, nple_ref[...]).astype(BF16)
    ple = _dot(p_ref[...].astype(BF16), wple_ref[...])
    x = x + _sigmoid(_dot(n2, wpg_ref[...])) * ple
    y_ref[...] = _rmsnorm(x, nfin_ref[...])


def _const_spec(shape, grid_rank=2):
    zeros = (0,) * len(shape)
    index_map = (lambda i: zeros) if grid_rank == 1 else (lambda b, j: zeros)
    return pl.BlockSpec(shape, index_map, pipeline_mode=pl.Buffered(1))


def _mixer_call(x, s0, c0, weights, *, nseq):
    nb, tokens, _ = x.shape
    t = TOKEN_BLOCK
    nj = tokens // t
    has_init = s0 is not None
    assert nseq == 1 or nj == 1
    state_spec = pl.BlockSpec((1, nseq, HEADS, HEAD_DIM, HEAD_DIM), lambda b, j: (b, 0, 0, 0, 0))
    conv_spec = pl.BlockSpec((1, nseq, CONV_W - 1, D_MODEL), lambda b, j: (b, 0, 0, 0))
    x_spec = pl.BlockSpec((1, t, D_MODEL), lambda b, j: (b, j, 0))
    in_specs = [x_spec]
    args = [x]
    if has_init:
        in_specs += [state_spec, conv_spec]
        args += [s0, c0]
    for w in weights:
        in_specs.append(_const_spec(w.shape))
        args.append(w)
    kern = functools.partial(_mixer_kernel, t=t, nseq=nseq, has_init=has_init, nj=nj)
    return pl.pallas_call(
        kern,
        grid=(nb, nj),
        in_specs=in_specs,
        out_specs=[x_spec, state_spec, conv_spec],
        out_shape=[
            jax.ShapeDtypeStruct(x.shape, F32),
            jax.ShapeDtypeStruct((nb, nseq, HEADS, HEAD_DIM, HEAD_DIM), F32),
            jax.ShapeDtypeStruct((nb, nseq, CONV_W - 1, D_MODEL), F32),
        ],
        scratch_shapes=[
            pltpu.VMEM((t, D_MODEL), BF16),
            pltpu.VMEM((N_SECTIONS - SIDE_SEC0, t, D_MODEL), F32),
            pltpu.VMEM((nseq * HEADS, HEAD_DIM, HEAD_DIM), F32),
            pltpu.VMEM((HALF, HALF), jnp.int32),
        ],
        compiler_params=pltpu.CompilerParams(
            dimension_semantics=("arbitrary", "arbitrary"),
            vmem_limit_bytes=VMEM_LIMIT_BYTES),
        name="mixer_init" if has_init else "mixer_zero",
    )(*args)


def _ffn_call(x, p, weights, name):
    tokens = x.shape[0]
    t = FFN_BLOCK
    assert tokens % t == 0
    x_spec = pl.BlockSpec((t, D_MODEL), lambda i: (i, 0))
    p_spec = pl.BlockSpec((t, PLE_DIM), lambda i: (i, 0))
    return pl.pallas_call(
        _ffn_kernel,
        grid=(tokens // t,),
        in_specs=[x_spec, p_spec] + [_const_spec(w.shape, 1) for w in weights],
        out_specs=x_spec,
        out_shape=jax.ShapeDtypeStruct(x.shape, F32),
        compiler_params=pltpu.CompilerParams(
            dimension_semantics=("arbitrary",),
            vmem_limit_bytes=VMEM_LIMIT_BYTES),
        name=name,
    )(x, p, *weights)


def kernel(x_prompt, x_sample, p_prompt, p_sample, state_hgrn, state_conv, lower_bounds, norm_mix, w_in, conv_w, hg_norm, w_branch_a, w_branch_b, w_out, norm_ffn, w_gate_up, w_down, norm_ple, w_ple, w_ple_gate, norm_final):
    depth = w_in.shape[0]
    assert depth == 1
    batch, seq, _ = x_prompt.shape
    dec_batch, dec_seq, _ = x_sample.shape
    assert seq % TOKEN_BLOCK == 0 and TOKEN_BLOCK % dec_seq == 0
    spb = TOKEN_BLOCK // dec_seq
    assert dec_batch % spb == 0
    nsb = dec_batch // spb

    row = lambda a: a.reshape(1, -1)
    mixer_w = [lower_bounds, row(norm_mix[0]), w_in[0].astype(BF16), conv_w[0], row(hg_norm[0]),
               w_branch_a[0].astype(BF16), w_branch_b[0].astype(BF16), w_out[0].astype(BF16)]
    ffn_w = [row(norm_ffn[0]), w_gate_up[0].astype(BF16), w_down[0].astype(BF16), row(norm_ple[0]),
             w_ple[0].astype(BF16), w_ple_gate[0].astype(BF16), row(norm_final)]

    hp, sp, cp = _mixer_call(x_prompt, None, None, mixer_w, nseq=1)
    xs = x_sample.reshape(nsb, TOKEN_BLOCK, D_MODEL)
    s0 = state_hgrn[0].reshape(nsb, spb, HEADS, HEAD_DIM, HEAD_DIM)
    c0 = state_conv[0].reshape(nsb, spb, CONV_W - 1, D_MODEL)
    hs, ss, cs = _mixer_call(xs, s0, c0, mixer_w, nseq=spb)

    y_prompt = _ffn_call(hp.reshape(-1, D_MODEL), p_prompt[0].reshape(-1, PLE_DIM), ffn_w,
                         "ffn_prompt")
    y_sample = _ffn_call(hs.reshape(-1, D_MODEL), p_sample[0].reshape(-1, PLE_DIM), ffn_w,
                         "ffn_sample")

    return (y_prompt.reshape(batch, seq, D_MODEL),
            y_sample.reshape(dec_batch, dec_seq, D_MODEL),
            sp.reshape(1, batch, HEADS, HEAD_DIM, HEAD_DIM),
            cp.reshape(1, batch, CONV_W - 1, D_MODEL),
            ss.reshape(1, dec_batch, HEADS, HEAD_DIM, HEAD_DIM),
            cs.reshape(1, dec_batch, CONV_W - 1, D_MODEL))
```

```python
import functools

import jax
import jax.numpy as jnp
from jax import lax
from jax.experimental import pallas as pl
from jax.experimental.pallas import tpu as pltpu

D_MODEL = 1024
HEADS = 8
HEAD_DIM = 128
CONV_W = 3
D_FF = 2816
PLE_DIM = 256
EPS = 1e-6
N_SECTIONS = 9
SIDE_SEC0 = 4
SUBLANES = 8
BF16_ROWS = 16
TOKEN_BLOCK = 256
HALF = TOKEN_BLOCK // 2
FFN_BLOCK = 1024
FFN_SUB = 256
VMEM_LIMIT_BYTES = 56 * 1024 * 1024

F32 = jnp.float32
BF16 = jnp.bfloat16


def _dot(a, b):
    return jnp.dot(a, b, preferred_element_type=F32)


def _dot_nt(a, b):
    return lax.dot_general(a, b, (((1,), (1,)), ((), ())), preferred_element_type=F32)


def _dot_tn(a, b):
    return lax.dot_general(a, b, (((0,), (0,)), ((), ())), preferred_element_type=F32)


def _rmsnorm(x, g):
    ms = jnp.mean(x * x, axis=-1, keepdims=True)
    return x * lax.rsqrt(ms + EPS) * g


def _sigmoid(x):
    return 1.0 / (1.0 + jnp.exp(-x))


def _level_step(c, blk):
    t, lanes = c.shape
    if blk >= SUBLANES:
        g = t // (2 * blk)
        c4 = c.reshape(g, 2, blk, lanes)
        ev = c4[:, 0]
        od = c4[:, 1]
        r = ev[:, blk - 1:blk, :]
        a = jnp.stack([r - ev, od], axis=1).reshape(t, lanes)
        cn = jnp.stack([ev, od + r], axis=1).reshape(t, lanes)
        return a, cn
    c3 = c.reshape(t // SUBLANES, SUBLANES, lanes)
    sub = lax.broadcasted_iota(jnp.int32, c3.shape, 1)
    if blk == 1:
        odd = (sub & 1) != 0
        prev = pltpu.roll(c3, 1, 1)
        a = jnp.where(odd, c3, 0.0).reshape(t, lanes)
        cn = jnp.where(odd, c3 + prev, c3).reshape(t, lanes)
        return a, cn
    r = None
    for grp in range(SUBLANES // (2 * blk)):
        row = grp * 2 * blk + blk - 1
        rg = jnp.broadcast_to(c3[:, row:row + 1, :], c3.shape)
        r = rg if r is None else jnp.where(sub >= grp * 2 * blk, rg, r)
    odd = (sub & blk) != 0
    a = jnp.where(odd, c3, r - c3).reshape(t, lanes)
    cn = jnp.where(odd, c3 + r, c3).reshape(t, lanes)
    return a, cn


def _run_lockstep(gens):
    results = [None] * len(gens)
    active = list(enumerate(gens))
    while active:
        still = []
        for i, g in active:
            try:
                next(g)
                still.append((i, g))
            except StopIteration as done:
                results[i] = done.value
        active = still
    return results


def _level_codes(t, seg):
    row = lax.broadcasted_iota(jnp.int32, (t, t), 0)
    col = lax.broadcasted_iota(jnp.int32, (t, t), 1)
    x = row ^ col
    cnt = jnp.zeros((t, t), jnp.int32)
    blk = 1
    while blk < seg:
        cnt = cnt + jnp.where(x >= blk, 1, 0)
        blk *= 2
    lower = jnp.where(row > col, jnp.where(x < seg, cnt, -1), -1)
    return jnp.where(row == col, 0, lower)


def _mixer_kernel(*refs, t, nseq, has_init, nj):
    if has_init:
        (x_ref, s0_ref, c0_ref, lbw_ref, nmix_ref, win_ref, convw_ref, hgn_ref, wa_ref, wb_ref,
         wout_ref, y_ref, sout_ref, cout_ref,
         n_scr, z_scr, st_scr, code_scr) = refs
    else:
        (x_ref, lbw_ref, nmix_ref, win_ref, convw_ref, hgn_ref, wa_ref, wb_ref,
         wout_ref, y_ref, sout_ref, cout_ref,
         n_scr, z_scr, st_scr, code_scr) = refs
        s0_ref = c0_ref = None
    seg = t // nseq
    b = pl.program_id(0)
    j = pl.program_id(1)

    @pl.when((b == 0) & (j == 0))
    def _():
        code_scr[...] = _level_codes(HALF, seg)

    @pl.when(j == 0)
    def _():
        for sq in range(nseq):
            for h in range(HEADS):
                if has_init:
                    st_scr[sq * HEADS + h] = s0_ref[0, sq, h].T
                else:
                    st_scr[sq * HEADS + h] = jnp.zeros((HEAD_DIM, HEAD_DIM), F32)
        if has_init:
            cout_ref[...] = c0_ref[...]
        else:
            cout_ref[...] = jnp.zeros(cout_ref.shape, F32)

    x = x_ref[0]
    n_scr[...] = _rmsnorm(x, nmix_ref[...]).astype(BF16)

    lbw = lbw_ref[...]
    mx = lbw[0:1]
    for r in range(1, lbw.shape[0]):
        mx = jnp.maximum(mx, lbw[r:r + 1])
    den = jnp.zeros_like(mx)
    for r in range(lbw.shape[0]):
        den = den + jnp.exp(lbw[r:r + 1] - mx)
    lb = jnp.exp(lbw[0:1] - mx) / den

    hgn = hgn_ref[...]
    code = code_scr[...]
    pair = 2 * HEAD_DIM

    def proj_cols(sec, c0):
        base = sec * D_MODEL + c0
        return _dot(n_scr[...], win_ref[:, base:base + pair])

    halves = [slice(hf * HALF, (hf + 1) * HALF) for hf in range(t // HALF)]

    n_pairs = HEADS // 2
    head_z = {}

    def head_proj(sec, pr):
        def run():
            head_z[sec, pr] = proj_cols(sec, pr * pair)
        return run

    def side_proj(sec, cb):
        def run():
            z_scr[sec - SIDE_SEC0, :, cb * pair:(cb + 1) * pair] = proj_cols(sec, cb * pair)
        return run

    fill_queue = []

    def fill():
        if fill_queue:
            fill_queue.pop(0)()

    def head(h, q, k, lf, v, sg):
        qb = q.astype(BF16)
        kb = k.astype(BF16)
        p, c_half = [], []
        for rows in halves:
            qh, qbh, kbh = q[rows], qb[rows], kb[rows]
            ph = jnp.where(code == 0, _dot_nt(qbh, kbh), 0.0)
            c = lf[rows]
            blk, lvl = 1, 1
            while blk < min(seg, HALF):
                a, c = _level_step(c, blk)
                xs = jnp.exp2(a)
                xb = xs.astype(BF16)
                kx = kbh * xb
                if blk >= SUBLANES:
                    g = HALF // (2 * blk)
                    odd = lambda z: z.reshape(g, 2, blk, HEAD_DIM)[:, 1].reshape(g * blk, HEAD_DIM)
                    if blk >= BF16_ROWS:
                        qx = odd(qbh) * odd(xb)
                    else:
                        qx = (odd(qh) * odd(xs)).astype(BF16)
                    s = _dot_nt(qx, kx).reshape(g, blk, HALF)
                    p4 = ph.reshape(g, 2, blk, HALF)
                    lvl_mask = code.reshape(g, 2, blk, HALF)[:, 1] == lvl
                    merged = jnp.where(lvl_mask, s, p4[:, 1])
                    ph = jnp.stack([p4[:, 0], merged], axis=1).reshape(HALF, HALF)
                else:
                    ph = jnp.where(code == lvl, _dot_nt(qbh * xb, kx), ph)
                blk *= 2
                lvl += 1
                yield
            p.append(ph)
            c_half.append(c)
        if seg > HALF:
            r = c_half[0][HALF - 1:HALF]
            x_hi = jnp.exp2(c_half[1]).astype(BF16)
            x_lo = jnp.exp2(r - c_half[0]).astype(BF16)
            p_low = _dot_nt(qb[halves[1]] * x_hi, kb[halves[0]] * x_lo)
            c = jnp.concatenate([c_half[0], c_half[1] + r], axis=0)
        else:
            p_low = None
            c = jnp.concatenate(c_half, axis=0)
        yield
        o_top =_dot(p[0].astype(BF16), v[halves[0]])
        if p_low is None:
            o_bot = _dot(p[1].astype(BF16), v[halves[1]])
        else:
            o_bot = _dot(jnp.concatenate([p_low, p[1]], axis=1).astype(BF16), v)
        o = jnp.concatenate([o_top, o_bot], axis=0)
        yield
        inter = []
        for sq in range(nseq):
            rows = slice(sq * seg, (sq + 1) * seg)
            cs = c[rows]
            tot = cs[seg - 1:seg]
            st = st_scr[sq * HEADS + h]
            qd = (q[rows] * jnp.exp2(cs)).astype(BF16)
            inter.append(_dot_nt(qd, st.astype(BF16)))
            kd = (k[rows] * jnp.exp2(tot - cs)).astype(BF16)
            st_scr[sq * HEADS + h] = st * jnp.exp2(tot) + _dot_tn(v[rows], kd)
        o = o + (inter[0] if nseq == 1 else jnp.concatenate(inter, axis=0))
        ms = jnp.mean(o * o, axis=-1, keepdims=True)
        return (o * lax.rsqrt(ms + EPS * HEAD_DIM) * hgn * sg).astype(BF16)

    oa_parts = []
    for sec in range(SIDE_SEC0):
        head_proj(sec, 0)()
    for pr in range(n_pairs):
        c0 = pr * pair
        if pr + 1 < n_pairs:
            fill_queue += [head_proj(sec, pr + 1) for sec in range(SIDE_SEC0)]
        fill_queue += [side_proj(sec, pr) for sec in range(SIDE_SEC0, N_SECTIONS)]
        zq = head_z.pop((0, pr))
        q2 = zq * _sigmoid(zq)
        sgf = _sigmoid(head_z.pop((1, pr)))
        lbp = lb[:, c0:c0 + pair]
        lf2 = jnp.log2(lbp + (1.0 - lbp) * sgf)
        k2 = (1.0 - lbp) * (1.0 - sgf)
        v2 = head_z.pop((2, pr)).astype(BF16)
        zg = head_z.pop((3, pr))
        sg2 = zg * _sigmoid(zg)
        gens = []
        for hh in range(2):
            cols = slice(hh * HEAD_DIM, (hh + 1) * HEAD_DIM)
            gens.append(head(2 * pr + hh, q2[:, cols], k2[:, cols], lf2[:, cols], v2[:, cols],
                             sg2[:, cols]))
        oa_parts += _run_lockstep(gens)
        while fill_queue:
            fill()

    oa = jnp.concatenate(oa_parts, axis=1)
    ya = _dot(oa, wa_ref[...])

    bg = z_scr[0]
    u = z_scr[1] * z_scr[2]
    cw = convw_ref[...]
    rowi = lax.broadcasted_iota(jnp.int32, (seg, D_MODEL), 0)
    conv_parts = []
    for sq in range(nseq):
        us = u[sq * seg:(sq + 1) * seg]
        prev = cout_ref[0, sq]
        s1 = jnp.where(rowi == 0, prev[1:2], pltpu.roll(us, 1, 0))
        s2 = jnp.where(rowi == 0, prev[0:1], jnp.where(rowi == 1, prev[1:2], pltpu.roll(us, 2, 0)))
        conv_parts.append(cw[0:1] * s2 + cw[1:2] * s1 + cw[2:3] * us)
        cout_ref[0, sq] = us[seg - (CONV_W - 1):seg]
    conv = conv_parts[0] if nseq == 1 else jnp.concatenate(conv_parts, axis=0)
    ob = (bg * conv).astype(BF16)

    mix = _sigmoid(z_scr[3]) * ya + _sigmoid(z_scr[4]) * _dot(ob, wb_ref[...])
    y_ref[0] = x + _dot(mix.astype(BF16), wout_ref[...])

    @pl.when(j == nj - 1)
    def _():
        for sq in range(nseq):
            for h in range(HEADS):
                sout_ref[0, sq, h] = st_scr[sq * HEADS + h].T


def _ffn_kernel(x_ref, p_ref, nffn_ref, wgu_ref, wdown_ref, nple_ref, wple_ref, wpg_ref, nfin_ref,
                y_ref):
    subs = [pl.ds(i * FFN_SUB, FFN_SUB) for i in range(FFN_BLOCK // FFN_SUB)]
    xs = [x_ref[r, :] for r in subs]
    ns = [_rmsnorm(x, nffn_ref[...]).astype(BF16) for x in xs]
    hidden = []
    for n in ns:
        gate = _dot(n, wgu_ref[:, 0:D_FF])
        up = _dot(n, wgu_ref[:, D_FF:2 * D_FF])
        hidden.append((gate * _sigmoid(gate) * up).astype(BF16))
    xs = [x + _dot(h, wdown_ref[...]) for x, h in zip(xs, hidden)]
    ples = [_dot(p_ref[r, :].astype(BF16), wple_ref[...]) for r in subs]
    n2s = [_rmsnorm(x, nple_ref[...]).astype(BF16) for x in xs]
    xs = [x + _sigmoid(_dot(n2, wpg_ref[...])) * ple for x, n2, ple in zip(xs, n2s, ples)]
    for r, x in zip(subs, xs):
        y_ref[r, :] = _rmsnorm(x, nfin_ref[...])


def _const_spec(shape, grid_rank=2):
    zeros = (0,) * len(shape)
    index_map = (lambda i: zeros) if grid_rank == 1 else (lambda b, j: zeros)
    return pl.BlockSpec(shape, index_map, pipeline_mode=pl.Buffered(1))


def _mixer_call(x, s0, c0, weights, *, nseq):
    nb, tokens, _ = x.shape
    t = TOKEN_BLOCK
    nj = tokens // t
    has_init = s0 is not None
    assert nseq == 1 or nj == 1
    state_spec = pl.BlockSpec((1, nseq, HEADS, HEAD_DIM, HEAD_DIM), lambda b, j: (b, 0, 0, 0, 0))
    conv_spec = pl.BlockSpec((1, nseq, CONV_W - 1, D_MODEL), lambda b, j: (b, 0, 0, 0))
    x_spec = pl.BlockSpec((1, t, D_MODEL), lambda b, j: (b, j, 0))
    in_specs = [x_spec]
    args = [x]
    if has_init:
        in_specs += [state_spec, conv_spec]
        args += [s0, c0]
    for w in weights:
        in_specs.append(_const_spec(w.shape))
        args.append(w)
    kern = functools.partial(_mixer_kernel, t=t, nseq=nseq, has_init=has_init, nj=nj)
    return pl.pallas_call(
        kern,
        grid=(nb, nj),
        in_specs=in_specs,
        out_specs=[x_spec, state_spec, conv_spec],
        out_shape=[
            jax.ShapeDtypeStruct(x.shape, F32),
            jax.ShapeDtypeStruct((nb, nseq, HEADS, HEAD_DIM, HEAD_DIM), F32),
            jax.ShapeDtypeStruct((nb, nseq, CONV_W - 1, D_MODEL), F32),
        ],
        scratch_shapes=[
            pltpu.VMEM((t, D_MODEL), BF16),
            pltpu.VMEM((N_SECTIONS - SIDE_SEC0, t, D_MODEL), F32),
            pltpu.VMEM((nseq * HEADS, HEAD_DIM, HEAD_DIM), F32),
            pltpu.VMEM((HALF, HALF), jnp.int32),
        ],
        compiler_params=pltpu.CompilerParams(
            dimension_semantics=("arbitrary", "arbitrary"),
            vmem_limit_bytes=VMEM_LIMIT_BYTES),
        name="mixer_init" if has_init else "mixer_zero",
    )(*args)


def _ffn_call(x, p, weights, name):
    tokens = x.shape[0]
    t = FFN_BLOCK
    assert tokens % t == 0
    x_spec = pl.BlockSpec((t, D_MODEL), lambda i: (i, 0))
    p_spec = pl.BlockSpec((t, PLE_DIM), lambda i: (i, 0))
    return pl.pallas_call(
        _ffn_kernel,
        grid=(tokens // t,),
        in_specs=[x_spec, p_spec] + [_const_spec(w.shape, 1) for w in weights],
        out_specs=x_spec,
        out_shape=jax.ShapeDtypeStruct(x.shape, F32),
        compiler_params=pltpu.CompilerParams(
            dimension_semantics=("arbitrary",),
            vmem_limit_bytes=VMEM_LIMIT_BYTES),
        name=name,
    )(x, p, *weights)


def kernel(x_prompt, x_sample, p_prompt, p_sample, state_hgrn, state_conv, lower_bounds, norm_mix, w_in, conv_w, hg_norm, w_branch_a, w_branch_b, w_out, norm_ffn, w_gate_up, w_down, norm_ple, w_ple, w_ple_gate, norm_final):
    depth = w_in.shape[0]
    assert depth == 1
    batch, seq, _ = x_prompt.shape
    dec_batch, dec_seq, _ = x_sample.shape
    assert seq % TOKEN_BLOCK == 0 and TOKEN_BLOCK % dec_seq == 0
    spb = TOKEN_BLOCK // dec_seq
    assert dec_batch % spb == 0
    nsb = dec_batch // spb

    row = lambda a: a.reshape(1, -1)
    mixer_w = [lower_bounds, row(norm_mix[0]), w_in[0].astype(BF16), conv_w[0], row(hg_norm[0]),
               w_branch_a[0].astype(BF16), w_branch_b[0].astype(BF16), w_out[0].astype(BF16)]
    ffn_w = [row(norm_ffn[0]), w_gate_up[0].astype(BF16), w_down[0].astype(BF16), row(norm_ple[0]),
             w_ple[0].astype(BF16), w_ple_gate[0].astype(BF16), row(norm_final)]

    hp, sp, cp = _mixer_call(x_prompt, None, None, mixer_w, nseq=1)
    xs = x_sample.reshape(nsb, TOKEN_BLOCK, D_MODEL)
    s0 = state_hgrn[0].reshape(nsb, spb, HEADS, HEAD_DIM, HEAD_DIM)
    c0 = state_conv[0].reshape(nsb, spb, CONV_W - 1, D_MODEL)
    hs, ss, cs = _mixer_call(xs, s0, c0, mixer_w, nseq=spb)

    y_prompt = _ffn_call(hp.reshape(-1, D_MODEL), p_prompt[0].reshape(-1, PLE_DIM), ffn_w,
                         "ffn_prompt")
    y_sample = _ffn_call(hs.reshape(-1, D_MODEL), p_sample[0].reshape(-1, PLE_DIM), ffn_w,
                         "ffn_sample")

    return (y_prompt.reshape(batch, seq, D_MODEL),
            y_sample.reshape(dec_batch, dec_seq, D_MODEL),
            sp.reshape(1, batch, HEADS, HEAD_DIM, HEAD_DIM),
            cp.reshape(1, batch, CONV_W - 1, D_MODEL),
            ss.reshape(1, dec_batch, HEADS, HEAD_DIM, HEAD_DIM),
            cs.reshape(1, dec_batch, CONV_W - 1, D_MODEL))
```

```python
import functools

import jax
import jax.numpy as jnp
from jax import lax
from jax.experimental import pallas as pl
from jax.experimental.pallas import tpu as pltpu

D_MODEL = 1024
HEADS = 8
HEAD_DIM = 128
CONV_W = 3
D_FF = 2816
PLE_DIM = 256
EPS = 1e-6
N_SECTIONS = 9
SIDE_SEC0 = 4
SUBLANES = 8
BF16_ROWS = 16
TOKEN_BLOCK = 256
HALF = TOKEN_BLOCK // 2
FFN_BLOCK = 1024
FFN_SUB = 256
VMEM_LIMIT_BYTES = 56 * 1024 * 1024

F32 = jnp.float32
BF16 = jnp.bfloat16


def _dot(a, b):
    return jnp.dot(a, b, preferred_element_type=F32)


def _dot_nt(a, b):
    return lax.dot_general(a, b, (((1,), (1,)), ((), ())), preferred_element_type=F32)


def _dot_tn(a, b):
    return lax.dot_general(a, b, (((0,), (0,)), ((), ())), preferred_element_type=F32)


def _rmsnorm(x, g):
    ms = jnp.mean(x * x, axis=-1, keepdims=True)
    return x * lax.rsqrt(ms + EPS) * g


def _sigmoid(x):
    return 1.0 / (1.0 + jnp.exp(-x))


def _level_step(c, blk):
    t, lanes = c.shape
    if blk >= SUBLANES:
        g = t // (2 * blk)
        c4 = c.reshape(g, 2, blk, lanes)
        ev = c4[:, 0]
        od = c4[:, 1]
        r = ev[:, blk - 1:blk, :]
        a = jnp.stack([r - ev, od], axis=1).reshape(t, lanes)
        cn = jnp.stack([ev, od + r], axis=1).reshape(t, lanes)
        return a, cn
    c3 = c.reshape(t // SUBLANES, SUBLANES, lanes)
    sub = lax.broadcasted_iota(jnp.int32, c3.shape, 1)
    if blk == 1:
        odd = (sub & 1) != 0
        prev = pltpu.roll(c3, 1, 1)
        a = jnp.where(odd, c3, 0.0).reshape(t, lanes)
        cn = jnp.where(odd, c3 + prev, c3).reshape(t, lanes)
        return a, cn
    r = None
    for grp in range(SUBLANES // (2 * blk)):
        row = grp * 2 * blk + blk - 1
        rg = jnp.broadcast_to(c3[:, row:row + 1, :], c3.shape)
        r = rg if r is None else jnp.where(sub >= grp * 2 * blk, rg, r)
    odd = (sub & blk) != 0
    a = jnp.where(odd, c3, r - c3).reshape(t, lanes)
    cn = jnp.where(odd, c3 + r, c3).reshape(t, lanes)
    return a, cn


def _level_codes(t, seg):
    row = lax.broadcasted_iota(jnp.int32, (t, t), 0)
    col = lax.broadcasted_iota(jnp.int32, (t, t), 1)
    x = row ^ col
    cnt = jnp.zeros((t, t), jnp.int32)
    blk = 1
    while blk < seg:
        cnt = cnt + jnp.where(x >= blk, 1, 0)
        blk *= 2
    lower = jnp.where(row > col, jnp.where(x < seg, cnt, -1), -1)
    return jnp.where(row == col, 0, lower)


def _mixer_kernel(*refs, t, nseq, has_init, nj):
    if has_init:
        (x_ref, s0_ref, c0_ref, lbw_ref, nmix_ref, win_ref, convw_ref, hgn_ref, wa_ref, wb_ref,
         wout_ref, y_ref, sout_ref, cout_ref,
         n_scr, z_scr, st_scr, code_scr) = refs
    else:
        (x_ref, lbw_ref, nmix_ref, win_ref, convw_ref, hgn_ref, wa_ref, wb_ref,
         wout_ref, y_ref, sout_ref, cout_ref,
         n_scr, z_scr, st_scr, code_scr) = refs
        s0_ref = c0_ref = None
    seg = t // nseq
    b = pl.program_id(0)
    j = pl.program_id(1)

    @pl.when((b == 0) & (j == 0))
    def _():
        code_scr[...] = _level_codes(HALF, seg)

    @pl.when(j == 0)
    def _():
        for sq in range(nseq):
            for h in range(HEADS):
                if has_init:
                    st_scr[sq * HEADS + h] = s0_ref[0, sq, h].T
                else:
                    st_scr[sq * HEADS + h] = jnp.zeros((HEAD_DIM, HEAD_DIM), F32)
        if has_init:
            cout_ref[...] = c0_ref[...]
        else:
            cout_ref[...] = jnp.zeros(cout_ref.shape, F32)

    x = x_ref[0]
    n_scr[...] = _rmsnorm(x, nmix_ref[...]).astype(BF16)

    lbw = lbw_ref[...]
    mx = lbw[0:1]
    for r in range(1, lbw.shape[0]):
        mx = jnp.maximum(mx, lbw[r:r + 1])
    den = jnp.zeros_like(mx)
    for r in range(lbw.shape[0]):
        den = den + jnp.exp(lbw[r:r + 1] - mx)
    lb = jnp.exp(lbw[0:1] - mx) / den

    hgn = hgn_ref[...]
    code = code_scr[...]
    pair = 2 * HEAD_DIM

    def proj_cols(sec, c0):
        base = sec * D_MODEL + c0
        return _dot(n_scr[...], win_ref[:, base:base + pair])

    halves = [slice(hf * HALF, (hf + 1) * HALF) for hf in range(t // HALF)]

    n_pairs = HEADS // 2
    head_z = {}

    def head_proj(sec, pr):
        def run():
            head_z[sec, pr] = proj_cols(sec, pr * pair)
        return run

    def side_proj(sec, cb):
        def run():
            z_scr[sec - SIDE_SEC0, :, cb * pair:(cb + 1) * pair] = proj_cols(sec, cb * pair)
        return run

    fill_queue = []

    def fill():
        if fill_queue:
            fill_queue.pop(0)()

    def head(h, q, k, lf, v, sg):
        qb = q.astype(BF16)
        kb = k.astype(BF16)
        p = [jnp.where(code == 0, _dot_nt(qb[rows], kb[rows]), 0.0) for rows in halves]
        p_low = None
        c = lf
        blk, lvl = 1, 1
        while blk < seg:
            a, c = _level_step(c, blk)
            xs = jnp.exp2(a)
            xb = xs.astype(BF16)
            if blk == HALF:
                p_low = _dot_nt(qb[halves[1]] * xb[halves[1]], kb[halves[0]] * xb[halves[0]])
            elif blk >= SUBLANES:
                g = HALF // (2 * blk)
                odd = lambda z: z.reshape(g, 2, blk, HEAD_DIM)[:, 1].reshape(g * blk, HEAD_DIM)
                kx = kb * xb
                for hf, rows in enumerate(halves):
                    if blk >= BF16_ROWS:
                        qx = odd(qb[rows]) * odd(xb[rows])
                    else:
                        qx = (odd(q[rows]) * odd(xs[rows])).astype(BF16)
                    s = _dot_nt(qx, kx[rows]).reshape(g, blk, HALF)
                    p4 = p[hf].reshape(g, 2, blk, HALF)
                    lvl_mask = code.reshape(g, 2, blk, HALF)[:, 1] == lvl
                    merged = jnp.where(lvl_mask, s, p4[:, 1])
                    p[hf] = jnp.stack([p4[:, 0], merged], axis=1).reshape(HALF, HALF)
            else:
                qx = qb * xb
                kx = kb * xb
                for hf, rows in enumerate(halves):
                    p[hf] = jnp.where(code == lvl, _dot_nt(qx[rows], kx[rows]), p[hf])
            blk *= 2
            lvl += 1
        o_top = _dot(p[0].astype(BF16), v[halves[0]])
        if p_low is None:
            o_bot = _dot(p[1].astype(BF16), v[halves[1]])
        else:
            o_bot = _dot(jnp.concatenate([p_low, p[1]], axis=1).astype(BF16), v)
        o = jnp.concatenate([o_top, o_bot], axis=0)
        inter = []
        for sq in range(nseq):
            rows = slice(sq * seg, (sq + 1) * seg)
            cs = c[rows]
            tot = cs[seg - 1:seg]
            st = st_scr[sq * HEADS + h]
            qd = (q[rows] * jnp.exp2(cs)).astype(BF16)
            inter.append(_dot_nt(qd, st.astype(BF16)))
            kd = (k[rows] * jnp.exp2(tot - cs)).astype(BF16)
            st_scr[sq * HEADS + h] = st * jnp.exp2(tot) + _dot_tn(v[rows], kd)
        o = o + (inter[0] if nseq == 1 else jnp.concatenate(inter, axis=0))
        ms = jnp.mean(o * o, axis=-1, keepdims=True)
        return (o * lax.rsqrt(ms + EPS * HEAD_DIM) * hgn * sg).astype(BF16)

    oa_parts = []
    for sec in range(SIDE_SEC0):
        head_proj(sec, 0)()
    for pr in range(n_pairs):
        c0 = pr * pair
        if pr + 1 < n_pairs:
            fill_queue += [head_proj(sec, pr + 1) for sec in range(SIDE_SEC0)]
        fill_queue += [side_proj(sec, pr) for sec in range(SIDE_SEC0, N_SECTIONS)]
        zq = head_z.pop((0, pr))
        q2 = zq * _sigmoid(zq)
        sgf = _sigmoid(head_z.pop((1, pr)))
        lbp = lb[:, c0:c0 + pair]
        lf2 = jnp.log2(lbp + (1.0 - lbp) * sgf)
        k2 = (1.0 - lbp) * (1.0 - sgf)
        v2 = head_z.pop((2, pr)).astype(BF16)
        zg = head_z.pop((3, pr))
        sg2 = zg * _sigmoid(zg)
        for hh in range(2):
            cols = slice(hh * HEAD_DIM, (hh + 1) * HEAD_DIM)
            oa_parts.append(head(2 * pr + hh, q2[:, cols], k2[:, cols], lf2[:, cols], v2[:, cols],
                                 sg2[:, cols]))
        while fill_queue:
            fill()

    oa = jnp.concatenate(oa_parts, axis=1)
    ya = _dot(oa, wa_ref[...])

    bg = z_scr[0]
    u = z_scr[1] * z_scr[2]
    cw = convw_ref[...]
    rowi = lax.broadcasted_iota(jnp.int32, (seg, D_MODEL), 0)
    conv_parts = []
    for sq in range(nseq):
        us = u[sq * seg:(sq + 1) * seg]
        prev = cout_ref[0, sq]
        s1 = jnp.where(rowi == 0, prev[1:2], pltpu.roll(us, 1, 0))
        s2 = jnp.where(rowi == 0, prev[0:1], jnp.where(rowi == 1, prev[1:2], pltpu.roll(us, 2, 0)))
        conv_parts.append(cw[0:1] * s2 + cw[1:2] * s1 + cw[2:3] * us)
        cout_ref[0, sq] = us[seg - (CONV_W - 1):seg]
    conv = conv_parts[0] if nseq == 1 else jnp.concatenate(conv_parts, axis=0)
    ob = (bg * conv).astype(BF16)

    mix = _sigmoid(z_scr[3]) * ya + _sigmoid(z_scr[4]) * _dot(ob, wb_ref[...])
    y_ref[0] = x + _dot(mix.astype(BF16), wout_ref[...])

    @pl.when(j == nj - 1)
    def _():
        for sq in range(nseq):
            for h in range(HEADS):
                sout_ref[0, sq, h] = st_scr[sq * HEADS + h].T


def _ffn_kernel(x_ref, p_ref, nffn_ref, wgu_ref, wdown_ref, nple_ref, wple_ref, wpg_ref, nfin_ref,
                y_ref):
    subs = [pl.ds(i * FFN_SUB, FFN_SUB) for i in range(x_ref.shape[0] // FFN_SUB)]
    xs = [x_ref[r, :] for r in subs]
    ns = [_rmsnorm(x, nffn_ref[...]).astype(BF16) for x in xs]
    hidden = []
    for n in ns:
        gate = _dot(n, wgu_ref[:, 0:D_FF])
        up = _dot(n, wgu_ref[:, D_FF:2 * D_FF])
        hidden.append((gate * _sigmoid(gate) * up).astype(BF16))
    xs = [x + _dot(h, wdown_ref[...]) for x, h in zip(xs, hidden)]
    ples = [_dot(p_ref[r, :].astype(BF16), wple_ref[...]) for r in subs]
    n2s = [_rmsnorm(x, nple_ref[...]).astype(BF16) for x in xs]
    xs = [x + _sigmoid(_dot(n2, wpg_ref[...])) * ple for x, n2, ple in zip(xs, n2s, ples)]
    for r, x in zip(subs, xs):
        y_ref[r, :] = _rmsnorm(x, nfin_ref[...])


def _const_spec(shape, grid_rank=2):
    zeros = (0,) * len(shape)
    index_map = (lambda i: zeros) if grid_rank == 1 else (lambda b, j: zeros)
    return pl.BlockSpec(shape, index_map, pipeline_mode=pl.Buffered(1))


def _mixer_call(x, s0, c0, weights, *, nseq):
    nb, tokens, _ = x.shape
    t = TOKEN_BLOCK
    nj = tokens // t
    has_init = s0 is not None
    assert nseq == 1 or nj == 1
    state_spec = pl.BlockSpec((1, nseq, HEADS, HEAD_DIM, HEAD_DIM), lambda b, j: (b, 0, 0, 0, 0))
    conv_spec = pl.BlockSpec((1, nseq, CONV_W - 1, D_MODEL), lambda b, j: (b, 0, 0, 0))
    x_spec = pl.BlockSpec((1, t, D_MODEL), lambda b, j: (b, j, 0))
    in_specs = [x_spec]
    args = [x]
    if has_init:
        in_specs += [state_spec, conv_spec]
        args += [s0, c0]
    for w in weights:
        in_specs.append(_const_spec(w.shape))
        args.append(w)
    kern = functools.partial(_mixer_kernel, t=t, nseq=nseq, has_init=has_init, nj=nj)
    return pl.pallas_call(
        kern,
        grid=(nb, nj),
        in_specs=in_specs,
        out_specs=[x_spec, state_spec, conv_spec],
        out_shape=[
            jax.ShapeDtypeStruct(x.shape, F32),
            jax.ShapeDtypeStruct((nb, nseq, HEADS, HEAD_DIM, HEAD_DIM), F32),
            jax.ShapeDtypeStruct((nb, nseq, CONV_W - 1, D_MODEL), F32),
        ],
        scratch_shapes=[
            pltpu.VMEM((t, D_MODEL), BF16),
            pltpu.VMEM((N_SECTIONS - SIDE_SEC0, t, D_MODEL), F32),
            pltpu.VMEM((nseq * HEADS, HEAD_DIM, HEAD_DIM), F32),
            pltpu.VMEM((HALF, HALF), jnp.int32),
        ],
        compiler_params=pltpu.CompilerParams(
            dimension_semantics=("arbitrary", "arbitrary"),
            vmem_limit_bytes=VMEM_LIMIT_BYTES),
        name="mixer_init" if has_init else "mixer_zero",
    )(*args)


def _ffn_call(x, p, weights, name):
    tokens = x.shape[0]
    t = FFN_BLOCK if tokens >= 4 * FFN_BLOCK else FFN_SUB
    assert tokens % t == 0 and t % FFN_SUB == 0
    x_spec = pl.BlockSpec((t, D_MODEL), lambda i: (i, 0))
    p_spec = pl.BlockSpec((t, PLE_DIM), lambda i: (i, 0))
    return pl.pallas_call(
        _ffn_kernel,
        grid=(tokens // t,),
        in_specs=[x_spec, p_spec] + [_const_spec(w.shape, 1) for w in weights],
        out_specs=x_spec,
        out_shape=jax.ShapeDtypeStruct(x.shape, F32),
        compiler_params=pltpu.CompilerParams(
            dimension_semantics=("arbitrary",),
            vmem_limit_bytes=VMEM_LIMIT_BYTES),
        name=name,
    )(x, p, *weights)


def kernel(x_prompt, x_sample, p_prompt, p_sample, state_hgrn, state_conv, lower_bounds, norm_mix, w_in, conv_w, hg_norm, w_branch_a, w_branch_b, w_out, norm_ffn, w_gate_up, w_down, norm_ple, w_ple, w_ple_gate, norm_final):
    depth = w_in.shape[0]
    assert depth == 1
    batch, seq, _ = x_prompt.shape
    dec_batch, dec_seq, _ = x_sample.shape
    assert seq % TOKEN_BLOCK == 0 and TOKEN_BLOCK % dec_seq == 0
    spb = TOKEN_BLOCK // dec_seq
    assert dec_batch % spb == 0
    nsb = dec_batch // spb

    row = lambda a: a.reshape(1, -1)
    mixer_w = [lower_bounds, row(norm_mix[0]), w_in[0].astype(BF16), conv_w[0], row(hg_norm[0]),
               w_branch_a[0].astype(BF16), w_branch_b[0].astype(BF16), w_out[0].astype(BF16)]
    ffn_w = [row(norm_ffn[0]), w_gate_up[0].astype(BF16), w_down[0].astype(BF16), row(norm_ple[0]),
             w_ple[0].astype(BF16), w_ple_gate[0].astype(BF16), row(norm_final)]

    hp, sp, cp = _mixer_call(x_prompt, None, None, mixer_w, nseq=1)
    xs = x_sample.reshape(nsb, TOKEN_BLOCK, D_MODEL)
    s0 = state_hgrn[0].reshape(nsb, spb, HEADS, HEAD_DIM, HEAD_DIM)
    c0 = state_conv[0].reshape(nsb, spb, CONV_W - 1, D_MODEL)
    hs, ss, cs = _mixer_call(xs, s0, c0, mixer_w, nseq=spb)

    y_prompt = _ffn_call(hp.reshape(-1, D_MODEL), p_prompt[0].reshape(-1, PLE_DIM), ffn_w,
                         "ffn_prompt")
    y_sample = _ffn_call(hs.reshape(-1, D_MODEL), p_sample[0].reshape(-1, PLE_DIM), ffn_w,
                         "ffn_sample")

    return (y_prompt.reshape(batch, seq, D_MODEL),
            y_sample.reshape(dec_batch, dec_seq, D_MODEL),
            sp.reshape(1, batch, HEADS, HEAD_DIM, HEAD_DIM),
            cp.reshape(1, batch, CONV_W - 1, D_MODEL),
            ss.reshape(1, dec_batch, HEADS, HEAD_DIM, HEAD_DIM),
            cs.reshape(1, dec_batch, CONV_W - 1, D_MODEL))
```

```python
import functools

import jax
import jax.numpy as jnp
from jax import lax
from jax.experimental import pallas as pl
from jax.experimental.pallas import tpu as pltpu

D_MODEL = 1024
HEADS = 8
HEAD_DIM = 128
CONV_W = 3
D_FF = 2816
PLE_DIM = 256
EPS = 1e-6
N_SECTIONS = 9
SIDE_SEC0 = 4
SUBLANES = 8
BF16_ROWS = 16
TOKEN_BLOCK = 256
HALF = TOKEN_BLOCK // 2
W_IN_CHUNK = 64
W_SQ_CHUNK = 256
FFN_BLOCK = 1024
FFN_SUB = 256
VMEM_LIMIT_BYTES = 56 * 1024 * 1024

F32 = jnp.float32
BF16 = jnp.bfloat16


def _dot(a, b):
    return jnp.dot(a, b, preferred_element_type=F32)


def _dot_nt(a, b):
    return lax.dot_general(a, b, (((1,), (1,)), ((), ())), preferred_element_type=F32)


def _dot_tn(a, b):
    return lax.dot_general(a, b, (((0,), (0,)), ((), ())), preferred_element_type=F32)


def _rmsnorm(x, g):
    ms = jnp.mean(x * x, axis=-1, keepdims=True)
    return x * lax.rsqrt(ms + EPS) * g


def _sigmoid(x):
    return 1.0 / (1.0 + jnp.exp(-x))


def _level_step(c, blk):
    t, lanes = c.shape
    if blk >= SUBLANES:
        g = t // (2 * blk)
        c4 = c.reshape(g, 2, blk, lanes)
        ev = c4[:, 0]
        od = c4[:, 1]
        r = ev[:, blk - 1:blk, :]
        a = jnp.stack([r - ev, od], axis=1).reshape(t, lanes)
        cn = jnp.stack([ev, od + r], axis=1).reshape(t, lanes)
        return a, cn
    c3 = c.reshape(t // SUBLANES, SUBLANES, lanes)
    sub = lax.broadcasted_iota(jnp.int32, c3.shape, 1)
    if blk == 1:
        odd = (sub & 1) != 0
        prev = pltpu.roll(c3, 1, 1)
        a = jnp.where(odd, c3, 0.0).reshape(t, lanes)
        cn = jnp.where(odd, c3 + prev, c3).reshape(t, lanes)
        return a, cn
    r = None
    for grp in range(SUBLANES // (2 * blk)):
        row = grp * 2 * blk + blk - 1
        rg = jnp.broadcast_to(c3[:, row:row + 1, :], c3.shape)
        r = rg if r is None else jnp.where(sub >= grp * 2 * blk, rg, r)
    odd = (sub & blk) != 0
    a = jnp.where(odd, c3, r - c3).reshape(t, lanes)
    cn = jnp.where(odd, c3 + r, c3).reshape(t, lanes)
    return a, cn


def _level_codes(t, seg):
    row = lax.broadcasted_iota(jnp.int32, (t, t), 0)
    col = lax.broadcasted_iota(jnp.int32, (t, t), 1)
    x = row ^ col
    cnt = jnp.zeros((t, t), jnp.int32)
    blk = 1
    while blk < seg:
        cnt = cnt + jnp.where(x >= blk, 1, 0)
        blk *= 2
    lower = jnp.where(row > col, jnp.where(x < seg, cnt, -1), -1)
    return jnp.where(row == col, 0, lower)


def _conversion_plan():
    return ([(W_IN_CHUNK, D_MODEL // W_IN_CHUNK)]
            + [(W_SQ_CHUNK, D_MODEL // W_SQ_CHUNK)] * 3)


def _mixer_kernel(*refs, t, nseq, has_init, nj, n_conv):
    i = pl.program_id(0)
    if has_init:
        step_refs = refs
    else:
        (x_ref, lbw_ref, nmix_ref, win32_ref, convw_ref, hgn_ref, wa32_ref, wb32_ref, wout32_ref,
         y_ref, sout_ref, cout_ref, win_out, wa_out, wb_out, wout_out,
         n_scr, z_scr, st_scr, code_scr, win_scr, wa_scr, wb_scr, wout_scr) = refs
        start = 0
        for src, out, dst, (rows, steps) in zip(
                (win32_ref, wa32_ref, wb32_ref, wout32_ref), (win_out, wa_out, wb_out, wout_out),
                (win_scr, wa_scr, wb_scr, wout_scr), _conversion_plan()):
            @pl.when((i >= start) & (i < start + steps))
            def _(src=src, out=out, dst=dst, rows=rows, start=start):
                chunk = src[...].astype(BF16)
                out[...] = chunk
                dst[pl.ds(pl.multiple_of((i - start) * rows, rows), rows), :] = chunk
            start += steps
        step_refs = (x_ref, lbw_ref, nmix_ref, win_scr, convw_ref, hgn_ref, wa_scr, wb_scr,
                     wout_scr, y_ref, sout_ref, cout_ref, n_scr, z_scr, st_scr, code_scr)
    step = i - n_conv

    @pl.when(step >= 0)
    def _():
        _mixer_step(step_refs, t=t, nseq=nseq, has_init=has_init, nj=nj,
                    b=step // nj, j=step % nj)


def _mixer_step(refs, *, t, nseq, has_init, nj, b, j):
    if has_init:
        (x_ref, s0_ref, c0_ref, lbw_ref, nmix_ref, win_ref, convw_ref, hgn_ref, wa_ref, wb_ref,
         wout_ref, y_ref, sout_ref, cout_ref,
         n_scr, z_scr, st_scr, code_scr) = refs
    else:
        (x_ref, lbw_ref, nmix_ref, win_ref, convw_ref, hgn_ref, wa_ref, wb_ref,
         wout_ref, y_ref, sout_ref, cout_ref,
         n_scr, z_scr, st_scr, code_scr) = refs
        s0_ref = c0_ref = None
    seg = t // nseq

    @pl.when((b == 0) & (j == 0))
    def _():
        code_scr[...] = _level_codes(HALF, seg)

    @pl.when(j == 0)
    def _():
        for sq in range(nseq):
            for h in range(HEADS):
                if has_init:
                    st_scr[sq * HEADS + h] = s0_ref[0, sq, h].T
                else:
                    st_scr[sq * HEADS + h] = jnp.zeros((HEAD_DIM, HEAD_DIM), F32)
        if has_init:
            cout_ref[...] = c0_ref[...]
        else:
            cout_ref[...] = jnp.zeros(cout_ref.shape, F32)

    x = x_ref[0]
    n_scr[...] = _rmsnorm(x, nmix_ref[...]).astype(BF16)

    lbw = lbw_ref[...]
    mx = lbw[0:1]
    for r in range(1, lbw.shape[0]):
        mx = jnp.maximum(mx, lbw[r:r + 1])
    den = jnp.zeros_like(mx)
    for r in range(lbw.shape[0]):
        den = den + jnp.exp(lbw[r:r + 1] - mx)
    lb = jnp.exp(lbw[0:1] - mx) / den

    hgn = hgn_ref[...]
    code = code_scr[...]
    pair = 2 * HEAD_DIM

    def proj_cols(sec, c0):
        base = sec * D_MODEL + c0
        return _dot(n_scr[...], win_ref[:, base:base + pair])

    halves = [slice(hf * HALF, (hf + 1) * HALF) for hf in range(t // HALF)]

    n_pairs = HEADS // 2
    head_z = {}

    def head_proj(sec, pr):
        def run():
            head_z[sec, pr] = proj_cols(sec, pr * pair)
        return run

    def side_proj(sec, cb):
        def run():
            z_scr[sec - SIDE_SEC0, :, cb * pair:(cb + 1) * pair] = proj_cols(sec, cb * pair)
        return run

    fill_queue = []

    def fill():
        if fill_queue:
            fill_queue.pop(0)()

    def head(h, q, k, lf, v, sg):
        qb = q.astype(BF16)
        kb = k.astype(BF16)
        p = [jnp.where(code == 0, _dot_nt(qb[rows], kb[rows]), 0.0) for rows in halves]
        p_low = None
        c = lf
        blk, lvl = 1, 1
        while blk < seg:
            a, c = _level_step(c, blk)
            xs = jnp.exp2(a)
            xb = xs.astype(BF16)
            if blk == HALF:
                p_low = _dot_nt(qb[halves[1]] * xb[halves[1]], kb[halves[0]] * xb[halves[0]])
            elif blk >= SUBLANES:
                g = HALF // (2 * blk)
                odd = lambda z: z.reshape(g, 2, blk, HEAD_DIM)[:, 1].reshape(g * blk, HEAD_DIM)
                kx = kb * xb
                for hf, rows in enumerate(halves):
                    if blk >= BF16_ROWS:
                        qx = odd(qb[rows]) * odd(xb[rows])
                    else:
                        qx = (odd(q[rows]) * odd(xs[rows])).astype(BF16)
                    s = _dot_nt(qx, kx[rows]).reshape(g, blk, HALF)
                    p4 = p[hf].reshape(g, 2, blk, HALF)
                    lvl_mask = code.reshape(g, 2, blk, HALF)[:, 1] == lvl
                    merged = jnp.where(lvl_mask, s, p4[:, 1])
                    p[hf] = jnp.stack([p4[:, 0], merged], axis=1).reshape(HALF, HALF)
            else:
                qx = qb * xb
                kx = kb * xb
                for hf, rows in enumerate(halves):
                    p[hf] = jnp.where(code == lvl, _dot_nt(qx[rows], kx[rows]), p[hf])
            blk *= 2
            lvl += 1
        o_top = _dot(p[0].astype(BF16), v[halves[0]])
        if p_low is None:
            o_bot = _dot(p[1].astype(BF16), v[halves[1]])
        else:
            o_bot = _dot(jnp.concatenate([p_low, p[1]], axis=1).astype(BF16), v)
        o = jnp.concatenate([o_top, o_bot], axis=0)
        inter = []
        for sq in range(nseq):
            rows = slice(sq * seg, (sq + 1) * seg)
            cs = c[rows]
            tot = cs[seg - 1:seg]
            st = st_scr[sq * HEADS + h]
            qd = (q[rows] * jnp.exp2(cs)).astype(BF16)
            inter.append(_dot_nt(qd, st.astype(BF16)))
            kd = (k[rows] * jnp.exp2(tot - cs)).astype(BF16)
            st_scr[sq * HEADS + h] = st * jnp.exp2(tot) + _dot_tn(v[rows], kd)
        o = o + (inter[0] if nseq == 1 else jnp.concatenate(inter, axis=0))
        ms = jnp.mean(o * o, axis=-1, keepdims=True)
        return (o * lax.rsqrt(ms + EPS * HEAD_DIM) * hgn * sg).astype(BF16)

    oa_parts = []
    for sec in range(SIDE_SEC0):
        head_proj(sec, 0)()
    for pr in range(n_pairs):
        c0 = pr * pair
        if pr + 1 < n_pairs:
            fill_queue += [head_proj(sec, pr + 1) for sec in range(SIDE_SEC0)]
        fill_queue += [side_proj(sec, pr) for sec in range(SIDE_SEC0, N_SECTIONS)]
        zq = head_z.pop((0, pr))
        q2 = zq * _sigmoid(zq)
        sgf = _sigmoid(head_z.pop((1, pr)))
        lbp = lb[:, c0:c0 + pair]
        lf2 = jnp.log2(lbp + (1.0 - lbp) * sgf)
        k2 = (1.0 - lbp) * (1.0 - sgf)
        v2 = head_z.pop((2, pr)).astype(BF16)
        zg = head_z.pop((3, pr))
        sg2 = zg * _sigmoid(zg)
        for hh in range(2):
            cols = slice(hh * HEAD_DIM, (hh + 1) * HEAD_DIM)
            oa_parts.append(head(2 * pr + hh, q2[:, cols], k2[:, cols], lf2[:, cols], v2[:, cols],
                                 sg2[:, cols]))
        while fill_queue:
            fill()

    oa = jnp.concatenate(oa_parts, axis=1)
    ya = _dot(oa, wa_ref[...])

    bg = z_scr[0]
    u = z_scr[1] * z_scr[2]
    cw = convw_ref[...]
    rowi = lax.broadcasted_iota(jnp.int32, (seg, D_MODEL), 0)
    conv_parts = []
    for sq in range(nseq):
        us = u[sq * seg:(sq + 1) * seg]
        prev = cout_ref[0, sq]
        s1 = jnp.where(rowi == 0, prev[1:2], pltpu.roll(us, 1, 0))
        s2 = jnp.where(rowi == 0, prev[0:1], jnp.where(rowi == 1, prev[1:2], pltpu.roll(us, 2, 0)))
        conv_parts.append(cw[0:1] * s2 + cw[1:2] * s1 + cw[2:3] * us)
        cout_ref[0, sq] = us[seg - (CONV_W - 1):seg]
    conv = conv_parts[0] if nseq == 1 else jnp.concatenate(conv_parts, axis=0)
    ob = (bg * conv).astype(BF16)

    mix = _sigmoid(z_scr[3]) * ya + _sigmoid(z_scr[4]) * _dot(ob, wb_ref[...])
    y_ref[0] = x + _dot(mix.astype(BF16), wout_ref[...])

    @pl.when(j == nj - 1)
    def _():
        for sq in range(nseq):
            for h in range(HEADS):
                sout_ref[0, sq, h] = st_scr[sq * HEADS + h].T


def _ffn_kernel(x_ref, p_ref, nffn_ref, wgu_ref, wdown_ref, nple_ref, wple_ref, wpg_ref, nfin_ref,
                y_ref):
    subs = [pl.ds(i * FFN_SUB, FFN_SUB) for i in range(x_ref.shape[0] // FFN_SUB)]
    xs = [x_ref[r, :] for r in subs]
    ns = [_rmsnorm(x, nffn_ref[...]).astype(BF16) for x in xs]
    hidden = []
    for n in ns:
        gate = _dot(n, wgu_ref[:, 0:D_FF])
        up = _dot(n, wgu_ref[:, D_FF:2 * D_FF])
        hidden.append((gate * _sigmoid(gate) * up).astype(BF16))
    xs = [x + _dot(h, wdown_ref[...]) for x, h in zip(xs, hidden)]
    ples = [_dot(p_ref[r, :].astype(BF16), wple_ref[...]) for r in subs]
    n2s = [_rmsnorm(x, nple_ref[...]).astype(BF16) for x in xs]
    xs = [x + _sigmoid(_dot(n2, wpg_ref[...])) * ple for x, n2, ple in zip(xs, n2s, ples)]
    for r, x in zip(subs, xs):
        y_ref[r, :] = _rmsnorm(x, nfin_ref[...])


def _const_spec(shape, grid_rank=2):
    zeros = (0,) * len(shape)
    index_map = (lambda i: zeros) if grid_rank == 1 else (lambda b, j: zeros)
    return pl.BlockSpec(shape, index_map, pipeline_mode=pl.Buffered(1))


def _mixer_call(x, s0, c0, weights, *, nseq):
    nb, tokens, _ = x.shape
    t = TOKEN_BLOCK
    nj = tokens // t
    has_init = s0 is not None
    assert nseq == 1 or nj == 1
    plan = [] if has_init else _conversion_plan()
    n_conv = sum(steps for _, steps in plan)

    def block_of(i):
        step = jnp.maximum(i - n_conv, 0)
        return step // nj, step % nj

    state_spec = pl.BlockSpec((1, nseq, HEADS, HEAD_DIM, HEAD_DIM),
                              lambda i: (block_of(i)[0], 0, 0, 0, 0))
    conv_spec = pl.BlockSpec((1, nseq, CONV_W - 1, D_MODEL), lambda i: (block_of(i)[0], 0, 0, 0))
    x_spec = pl.BlockSpec((1, t, D_MODEL), lambda i: block_of(i) + (0,))
    in_specs = [x_spec]
    args = [x]
    if has_init:
        in_specs += [state_spec, conv_spec]
        args += [s0, c0]
    out_specs = [x_spec, state_spec, conv_spec]
    out_shape = [
        jax.ShapeDtypeStruct(x.shape, F32),
        jax.ShapeDtypeStruct((nb, nseq, HEADS, HEAD_DIM, HEAD_DIM), F32),
        jax.ShapeDtypeStruct((nb, nseq, CONV_W - 1, D_MODEL), F32),
    ]
    scratch_shapes = [
        pltpu.VMEM((t, D_MODEL), BF16),
        pltpu.VMEM((N_SECTIONS - SIDE_SEC0, t, D_MODEL), F32),
        pltpu.VMEM((nseq * HEADS, HEAD_DIM, HEAD_DIM), F32),
        pltpu.VMEM((HALF, HALF), jnp.int32),
    ]
    start = 0
    big = iter(plan)
    for w in weights:
        if w.dtype == F32 and w.shape[0] == D_MODEL and not has_init:
            rows, steps = next(big)
            chunk_spec = pl.BlockSpec(
                (rows, w.shape[1]),
                lambda i, start=start, steps=steps: (jnp.clip(i - start, 0, steps - 1), 0))
            in_specs.append(chunk_spec)
            out_specs.append(chunk_spec)
            out_shape.append(jax.ShapeDtypeStruct(w.shape, BF16))
            scratch_shapes.append(pltpu.VMEM(w.shape, BF16))
            start += steps
        else:
            assert w.dtype == BF16 or w.shape[0] < D_MODEL
            in_specs.append(_const_spec(w.shape, 1))
        args.append(w)
    kern = functools.partial(_mixer_kernel, t=t, nseq=nseq, has_init=has_init, nj=nj,
                             n_conv=n_conv)
    return pl.pallas_call(
        kern,
        grid=(n_conv + nb * nj,),
        in_specs=in_specs,
        out_specs=out_specs,
        out_shape=out_shape,
        scratch_shapes=scratch_shapes,
        compiler_params=pltpu.CompilerParams(
            dimension_semantics=("arbitrary",),
            vmem_limit_bytes=VMEM_LIMIT_BYTES),
        name="mixer_init" if has_init else "mixer_zero",
    )(*args)


def _ffn_call(x, p, weights, name):
    tokens = x.shape[0]
    t = FFN_BLOCK if tokens >= 4 * FFN_BLOCK else FFN_SUB
    assert tokens % t == 0 and t % FFN_SUB == 0
    x_spec = pl.BlockSpec((t, D_MODEL), lambda i: (i, 0))
    p_spec = pl.BlockSpec((t, PLE_DIM), lambda i: (i, 0))
    return pl.pallas_call(
        _ffn_kernel,
        grid=(tokens // t,),
        in_specs=[x_spec, p_spec] + [_const_spec(w.shape, 1) for w in weights],
        out_specs=x_spec,
        out_shape=jax.ShapeDtypeStruct(x.shape, F32),
        compiler_params=pltpu.CompilerParams(
            dimension_semantics=("arbitrary",),
            vmem_limit_bytes=VMEM_LIMIT_BYTES),
        name=name,
    )(x, p, *weights)


def kernel(x_prompt, x_sample, p_prompt, p_sample, state_hgrn, state_conv, lower_bounds, norm_mix, w_in, conv_w, hg_norm, w_branch_a, w_branch_b, w_out, norm_ffn, w_gate_up, w_down, norm_ple, w_ple, w_ple_gate, norm_final):
    depth = w_in.shape[0]
    assert depth == 1
    batch, seq, _ = x_prompt.shape
    dec_batch, dec_seq, _ = x_sample.shape
    assert seq % TOKEN_BLOCK == 0 and TOKEN_BLOCK % dec_seq == 0
    spb = TOKEN_BLOCK // dec_seq
    assert dec_batch % spb == 0
    nsb = dec_batch // spb

    row = lambda a: a.reshape(1, -1)
    mixer_w = [lower_bounds, row(norm_mix[0]), w_in[0], conv_w[0], row(hg_norm[0]),
               w_branch_a[0], w_branch_b[0], w_out[0]]
    ffn_w = [row(norm_ffn[0]), w_gate_up[0].astype(BF16), w_down[0].astype(BF16), row(norm_ple[0]),
             w_ple[0].astype(BF16), w_ple_gate[0].astype(BF16), row(norm_final)]

    hp, sp, cp, win_bf, wa_bf, wb_bf, wout_bf = _mixer_call(x_prompt, None, None, mixer_w, nseq=1)
    mixer_w_bf = [lower_bounds, row(norm_mix[0]), win_bf, conv_w[0], row(hg_norm[0]),
                  wa_bf, wb_bf, wout_bf]
    xs = x_sample.reshape(nsb, TOKEN_BLOCK, D_MODEL)
    s0 = state_hgrn[0].reshape(nsb, spb, HEADS, HEAD_DIM, HEAD_DIM)
    c0 = state_conv[0].reshape(nsb, spb, CONV_W - 1, D_MODEL)
    hs, ss, cs = _mixer_call(xs, s0, c0, mixer_w_bf, nseq=spb)

    y_prompt = _ffn_call(hp.reshape(-1, D_MODEL), p_prompt[0].reshape(-1, PLE_DIM), ffn_w,
                         "ffn_prompt")
    y_sample = _ffn_call(hs.reshape(-1, D_MODEL), p_sample[0].reshape(-1, PLE_DIM), ffn_w,
                         "ffn_sample")

    return (y_prompt.reshape(batch, seq, D_MODEL),
            y_sample.reshape(dec_batch, dec_seq, D_MODEL),
            sp.reshape(1, batch, HEADS, HEAD_DIM, HEAD_DIM),
            cp.reshape(1, batch, CONV_W - 1, D_MODEL),
            ss.reshape(1, dec_batch, HEADS, HEAD_DIM, HEAD_DIM),
            cs.reshape(1, dec_batch, CONV_W - 1, D_MODEL))
```

```python
import functools

import jax
import jax.numpy as jnp
from jax import lax
from jax.experimental import pallas as pl
from jax.experimental.pallas import tpu as pltpu

D_MODEL = 1024
HEADS = 8
HEAD_DIM = 128
CONV_W = 3
D_FF = 2816
PLE_DIM = 256
EPS = 1e-6
N_SECTIONS = 9
SIDE_SEC0 = 4
SUBLANES = 8
BF16_ROWS = 16
SEGMENT = 256
HALF = SEGMENT // 2
MIXER_BLOCK = 512
FFN_BLOCK = 1024
FFN_SUB = 256
VMEM_LIMIT_BYTES = 56 * 1024 * 1024

F32 = jnp.float32
BF16 = jnp.bfloat16


def _dot(a, b):
    return jnp.dot(a, b, preferred_element_type=F32)


def _dot_nt(a, b):
    return lax.dot_general(a, b, (((1,), (1,)), ((), ())), preferred_element_type=F32)


def _dot_tn(a, b):
    return lax.dot_general(a, b, (((0,), (0,)), ((), ())), preferred_element_type=F32)


def _rmsnorm(x, g):
    ms = jnp.mean(x * x, axis=-1, keepdims=True)
    return x * lax.rsqrt(ms + EPS) * g


def _sigmoid(x):
    return 1.0 / (1.0 + jnp.exp(-x))


def _level_step(c, blk):
    t, lanes = c.shape
    if blk >= SUBLANES:
        g = t // (2 * blk)
        c4 = c.reshape(g, 2, blk, lanes)
        ev = c4[:, 0]
        od = c4[:, 1]
        r = ev[:, blk - 1:blk, :]
        a = jnp.stack([r - ev, od], axis=1).reshape(t, lanes)
        cn = jnp.stack([ev, od + r], axis=1).reshape(t, lanes)
        return a, cn
    c3 = c.reshape(t // SUBLANES, SUBLANES, lanes)
    sub = lax.broadcasted_iota(jnp.int32, c3.shape, 1)
    if blk == 1:
        odd = (sub & 1) != 0
        prev = pltpu.roll(c3, 1, 1)
        a = jnp.where(odd, c3, 0.0).reshape(t, lanes)
        cn = jnp.where(odd, c3 + prev, c3).reshape(t, lanes)
        return a, cn
    r = None
    for grp in range(SUBLANES // (2 * blk)):
        row = grp * 2 * blk + blk - 1
        rg = jnp.broadcast_to(c3[:, row:row + 1, :], c3.shape)
        r = rg if r is None else jnp.where(sub >= grp * 2 * blk, rg, r)
    odd = (sub & blk) != 0
    a = jnp.where(odd, c3, r - c3).reshape(t, lanes)
    cn = jnp.where(odd, c3 + r, c3).reshape(t, lanes)
    return a, cn


def _level_codes(t, seg):
    row = lax.broadcasted_iota(jnp.int32, (t, t), 0)
    col = lax.broadcasted_iota(jnp.int32, (t, t), 1)
    x = row ^ col
    cnt = jnp.zeros((t, t), jnp.int32)
    blk = 1
    while blk < seg:
        cnt = cnt + jnp.where(x >= blk, 1, 0)
        blk *= 2
    lower = jnp.where(row > col, jnp.where(x < seg, cnt, -1), -1)
    return jnp.where(row == col, 0, lower)


def _mixer_kernel(*refs, t, nseq, nseg, has_init, nj):
    if has_init:
        (x_ref, s0_ref, c0_ref, lbw_ref, nmix_ref, win_ref, convw_ref, hgn_ref, wa_ref, wb_ref,
         wout_ref, y_ref, sout_ref, cout_ref,
         n_scr, z_scr, st_scr, code_scr) = refs
    else:
        (x_ref, lbw_ref, nmix_ref, win_ref, convw_ref, hgn_ref, wa_ref, wb_ref,
         wout_ref, y_ref, sout_ref, cout_ref,
         n_scr, z_scr, st_scr, code_scr) = refs
        s0_ref = c0_ref = None
    seg = t // nseg
    conv_len = t // nseq
    b = pl.program_id(0)
    j = pl.program_id(1)

    @pl.when((b == 0) & (j == 0))
    def _():
        code_scr[...] = _level_codes(HALF, seg)

    @pl.when(j == 0)
    def _():
        for sq in range(nseq):
            for h in range(HEADS):
                if has_init:
                    st_scr[sq * HEADS + h] = s0_ref[0, sq, h].T
                else:
                    st_scr[sq * HEADS + h] = jnp.zeros((HEAD_DIM, HEAD_DIM), F32)
        if has_init:
            cout_ref[...] = c0_ref[...]
        else:
            cout_ref[...] = jnp.zeros(cout_ref.shape, F32)

    x = x_ref[0]
    n_scr[...] = _rmsnorm(x, nmix_ref[...]).astype(BF16)

    lbw = lbw_ref[...]
    mx = lbw[0:1]
    for r in range(1, lbw.shape[0]):
        mx = jnp.maximum(mx, lbw[r:r + 1])
    den = jnp.zeros_like(mx)
    for r in range(lbw.shape[0]):
        den = den + jnp.exp(lbw[r:r + 1] - mx)
    lb = jnp.exp(lbw[0:1] - mx) / den

    hgn = hgn_ref[...]
    code = code_scr[...]
    pair = 2 * HEAD_DIM

    def proj_cols(sec, c0):
        base = sec * D_MODEL + c0
        return _dot(n_scr[...], win_ref[:, base:base + pair])

    halves = [slice(hf * HALF, (hf + 1) * HALF) for hf in range(t // HALF)]

    n_pairs = HEADS // 2
    head_z = {}

    def head_proj(sec, pr):
        def run():
            head_z[sec, pr] = proj_cols(sec, pr * pair)
        return run

    def side_proj(sec, cb):
        def run():
            z_scr[sec - SIDE_SEC0, :, cb * pair:(cb + 1) * pair] = proj_cols(sec, cb * pair)
        return run

    fill_queue = []

    def fill():
        if fill_queue:
            fill_queue.pop(0)()

    def head(h, q, k, lf, v, sg):
        qb = q.astype(BF16)
        kb = k.astype(BF16)
        p = [jnp.where(code == 0, _dot_nt(qb[rows], kb[rows]), 0.0) for rows in halves]
        p_low = {}
        c = lf
        blk, lvl = 1, 1
        while blk < seg:
            a, c = _level_step(c, blk)
            xs = jnp.exp2(a)
            xb = xs.astype(BF16)
            if blk == HALF:
                for hf in range(1, len(halves), 2):
                    hi, lo = halves[hf], halves[hf - 1]
                    p_low[hf] = _dot_nt(qb[hi] * xb[hi], kb[lo] * xb[lo])
            elif blk >= SUBLANES:
                g = HALF // (2 * blk)
                odd = lambda z: z.reshape(g, 2, blk, HEAD_DIM)[:, 1].reshape(g * blk, HEAD_DIM)
                kx = kb * xb
                for hf, rows in enumerate(halves):
                    if blk >= BF16_ROWS:
                        qx = odd(qb[rows]) * odd(xb[rows])
                    else:
                        qx = (odd(q[rows]) * odd(xs[rows])).astype(BF16)
                    s = _dot_nt(qx, kx[rows]).reshape(g, blk, HALF)
                    p4 = p[hf].reshape(g, 2, blk, HALF)
                    lvl_mask = code.reshape(g, 2, blk, HALF)[:, 1] == lvl
                    merged = jnp.where(lvl_mask, s, p4[:, 1])
                    p[hf] = jnp.stack([p4[:, 0], merged], axis=1).reshape(HALF, HALF)
            else:
                qx = qb * xb
                kx = kb * xb
                for hf, rows in enumerate(halves):
                    p[hf] = jnp.where(code == lvl, _dot_nt(qx[rows], kx[rows]), p[hf])
            blk *= 2
            lvl += 1
        o_parts = []
        for hf, rows in enumerate(halves):
            if hf in p_low:
                both = slice(halves[hf - 1].start, rows.stop)
                o_parts.append(_dot(jnp.concatenate([p_low[hf], p[hf]], axis=1).astype(BF16),
                                    v[both]))
            else:
                o_parts.append(_dot(p[hf].astype(BF16), v[rows]))
        o = jnp.concatenate(o_parts, axis=0)
        inter = []
        for si in range(nseg):
            rows = slice(si * seg, (si + 1) * seg)
            slot = (si if nseq > 1 else 0) * HEADS + h
            cs = c[rows]
            tot = cs[seg - 1:seg]
            st = st_scr[slot]
            qd = (q[rows] * jnp.exp2(cs)).astype(BF16)
            inter.append(_dot_nt(qd, st.astype(BF16)))
            kd = (k[rows] * jnp.exp2(tot - cs)).astype(BF16)
            st_scr[slot] = st * jnp.exp2(tot) + _dot_tn(v[rows], kd)
        o = o + (inter[0] if nseg == 1 else jnp.concatenate(inter, axis=0))
        ms = jnp.mean(o * o, axis=-1, keepdims=True)
        return (o * lax.rsqrt(ms + EPS * HEAD_DIM) * hgn * sg).astype(BF16)

    oa_parts = []
    for sec in range(SIDE_SEC0):
        head_proj(sec, 0)()
    for pr in range(n_pairs):
        c0 = pr * pair
        if pr + 1 < n_pairs:
            fill_queue += [head_proj(sec, pr + 1) for sec in range(SIDE_SEC0)]
        fill_queue += [side_proj(sec, pr) for sec in range(SIDE_SEC0, N_SECTIONS)]
        zq = head_z.pop((0, pr))
        q2 = zq * _sigmoid(zq)
        sgf = _sigmoid(head_z.pop((1, pr)))
        lbp = lb[:, c0:c0 + pair]
        lf2 = jnp.log2(lbp + (1.0 - lbp) * sgf)
        k2 = (1.0 - lbp) * (1.0 - sgf)
        v2 = head_z.pop((2, pr)).astype(BF16)
        zg = head_z.pop((3, pr))
        sg2 = zg * _sigmoid(zg)
        for hh in range(2):
            cols = slice(hh * HEAD_DIM, (hh + 1) * HEAD_DIM)
            oa_parts.append(head(2 * pr + hh, q2[:, cols], k2[:, cols], lf2[:, cols], v2[:, cols],
                                 sg2[:, cols]))
        while fill_queue:
            fill()

    oa = jnp.concatenate(oa_parts, axis=1)
    ya = _dot(oa, wa_ref[...])

    bg = z_scr[0]
    u = z_scr[1] * z_scr[2]
    cw = convw_ref[...]
    rowi = lax.broadcasted_iota(jnp.int32, (conv_len, D_MODEL), 0)
    conv_parts = []
    for sq in range(nseq):
        us = u[sq * conv_len:(sq + 1) * conv_len]
        prev = cout_ref[0, sq]
        s1 = jnp.where(rowi == 0, prev[1:2], pltpu.roll(us, 1, 0))
        s2 = jnp.where(rowi == 0, prev[0:1], jnp.where(rowi == 1, prev[1:2], pltpu.roll(us, 2, 0)))
        conv_parts.append(cw[0:1] * s2 + cw[1:2] * s1 + cw[2:3] * us)
        cout_ref[0, sq] = us[conv_len - (CONV_W - 1):conv_len]
    conv = conv_parts[0] if nseq == 1 else jnp.concatenate(conv_parts, axis=0)
    ob = (bg * conv).astype(BF16)

    mix = _sigmoid(z_scr[3]) * ya + _sigmoid(z_scr[4]) * _dot(ob, wb_ref[...])
    y_ref[0] = x + _dot(mix.astype(BF16), wout_ref[...])

    @pl.when(j == nj - 1)
    def _():
        for sq in range(nseq):
            for h in range(HEADS):
                sout_ref[0, sq, h] = st_scr[sq * HEADS + h].T


def _ffn_kernel(x_ref, p_ref, nffn_ref, wgu_ref, wdown_ref, nple_ref, wple_ref, wpg_ref, nfin_ref,
                y_ref):
    subs = [pl.ds(i * FFN_SUB, FFN_SUB) for i in range(x_ref.shape[0] // FFN_SUB)]
    xs = [x_ref[r, :] for r in subs]
    ns = [_rmsnorm(x, nffn_ref[...]).astype(BF16) for x in xs]
    hidden = []
    for n in ns:
        gate = _dot(n, wgu_ref[:, 0:D_FF])
        up = _dot(n, wgu_ref[:, D_FF:2 * D_FF])
        hidden.append((gate * _sigmoid(gate) * up).astype(BF16))
    xs = [x + _dot(h, wdown_ref[...]) for x, h in zip(xs, hidden)]
    ples = [_dot(p_ref[r, :].astype(BF16), wple_ref[...]) for r in subs]
    n2s = [_rmsnorm(x, nple_ref[...]).astype(BF16) for x in xs]
    xs = [x + _sigmoid(_dot(n2, wpg_ref[...])) * ple for x, n2, ple in zip(xs, n2s, ples)]
    for r, x in zip(subs, xs):
        y_ref[r, :] = _rmsnorm(x, nfin_ref[...])


def _const_spec(shape, grid_rank=2):
    zeros = (0,) * len(shape)
    index_map = (lambda i: zeros) if grid_rank == 1 else (lambda b, j: zeros)
    return pl.BlockSpec(shape, index_map, pipeline_mode=pl.Buffered(1))


def _mixer_call(x, s0, c0, weights, *, t, nseq, nseg):
    nb, tokens, _ = x.shape
    nj = tokens // t
    has_init = s0 is not None
    assert tokens % t == 0 and (nseq == 1 or nj == 1) and (nseq == 1 or nseq == nseg)
    assert (t // nseg) in (SEGMENT, SEGMENT // 4) and t % HALF == 0
    state_spec = pl.BlockSpec((1, nseq, HEADS, HEAD_DIM, HEAD_DIM), lambda b, j: (b, 0, 0, 0, 0))
    conv_spec = pl.BlockSpec((1, nseq, CONV_W - 1, D_MODEL), lambda b, j: (b, 0, 0, 0))
    x_spec = pl.BlockSpec((1, t, D_MODEL), lambda b, j: (b, j, 0))
    in_specs = [x_spec]
    args = [x]
    if has_init:
        in_specs += [state_spec, conv_spec]
        args += [s0, c0]
    for w in weights:
        in_specs.append(_const_spec(w.shape))
        args.append(w)
    kern = functools.partial(_mixer_kernel, t=t, nseq=nseq, nseg=nseg, has_init=has_init, nj=nj)
    return pl.pallas_call(
        kern,
        grid=(nb, nj),
        in_specs=in_specs,
        out_specs=[x_spec, state_spec, conv_spec],
        out_shape=[
            jax.ShapeDtypeStruct(x.shape, F32),
            jax.ShapeDtypeStruct((nb, nseq, HEADS, HEAD_DIM, HEAD_DIM), F32),
            jax.ShapeDtypeStruct((nb, nseq, CONV_W - 1, D_MODEL), F32),
        ],
        scratch_shapes=[
            pltpu.VMEM((t, D_MODEL), BF16),
            pltpu.VMEM((N_SECTIONS - SIDE_SEC0, t, D_MODEL), F32),
            pltpu.VMEM((nseq * HEADS, HEAD_DIM, HEAD_DIM), F32),
            pltpu.VMEM((HALF, HALF), jnp.int32),
        ],
        compiler_params=pltpu.CompilerParams(
            dimension_semantics=("arbitrary", "arbitrary"),
            vmem_limit_bytes=VMEM_LIMIT_BYTES),
        name="mixer_init" if has_init else "mixer_zero",
    )(*args)


def _ffn_call(x, p, weights, name):
    tokens = x.shape[0]
    t = FFN_BLOCK if tokens >= 4 * FFN_BLOCK else FFN_SUB
    assert tokens % t == 0 and t % FFN_SUB == 0
    x_spec = pl.BlockSpec((t, D_MODEL), lambda i: (i, 0))
    p_spec = pl.BlockSpec((t, PLE_DIM), lambda i: (i, 0))
    return pl.pallas_call(
        _ffn_kernel,
        grid=(tokens // t,),
        in_specs=[x_spec, p_spec] + [_const_spec(w.shape, 1) for w in weights],
        out_specs=x_spec,
        out_shape=jax.ShapeDtypeStruct(x.shape, F32),
        compiler_params=pltpu.CompilerParams(
            dimension_semantics=("arbitrary",),
            vmem_limit_bytes=VMEM_LIMIT_BYTES),
        name=name,
    )(x, p, *weights)


def kernel(x_prompt, x_sample, p_prompt, p_sample, state_hgrn, state_conv, lower_bounds, norm_mix, w_in, conv_w, hg_norm, w_branch_a, w_branch_b, w_out, norm_ffn, w_gate_up, w_down, norm_ple, w_ple, w_ple_gate, norm_final):
    depth = w_in.shape[0]
    assert depth == 1
    batch, seq, _ = x_prompt.shape
    dec_batch, dec_seq, _ = x_sample.shape
    assert seq % MIXER_BLOCK == 0 and SEGMENT % dec_seq == 0
    spb = SEGMENT // dec_seq
    assert dec_batch % spb == 0
    nsb = dec_batch // spb

    row = lambda a: a.reshape(1, -1)
    mixer_w = [lower_bounds, row(norm_mix[0]), w_in[0].astype(BF16), conv_w[0], row(hg_norm[0]),
               w_branch_a[0].astype(BF16), w_branch_b[0].astype(BF16), w_out[0].astype(BF16)]
    ffn_w = [row(norm_ffn[0]), w_gate_up[0].astype(BF16), w_down[0].astype(BF16), row(norm_ple[0]),
             w_ple[0].astype(BF16), w_ple_gate[0].astype(BF16), row(norm_final)]

    hp, sp, cp = _mixer_call(x_prompt, None, None, mixer_w, t=MIXER_BLOCK, nseq=1,
                             nseg=MIXER_BLOCK // SEGMENT)
    xs = x_sample.reshape(nsb, SEGMENT, D_MODEL)
    s0 = state_hgrn[0].reshape(nsb, spb, HEADS, HEAD_DIM, HEAD_DIM)
    c0 = state_conv[0].reshape(nsb, spb, CONV_W - 1, D_MODEL)
    hs, ss, cs = _mixer_call(xs, s0, c0, mixer_w, t=SEGMENT, nseq=spb, nseg=spb)

    y_prompt = _ffn_call(hp.reshape(-1, D_MODEL), p_prompt[0].reshape(-1, PLE_DIM), ffn_w,
                         "ffn_prompt")
    y_sample = _ffn_call(hs.reshape(-1, D_MODEL), p_sample[0].reshape(-1, PLE_DIM), ffn_w,
                         "ffn_sample")

    return (y_prompt.reshape(batch, seq, D_MODEL),
            y_sample.reshape(dec_batch, dec_seq, D_MODEL),
            sp.reshape(1, batch, HEADS, HEAD_DIM, HEAD_DIM),
            cp.reshape(1, batch, CONV_W - 1, D_MODEL),
            ss.reshape(1, dec_batch, HEADS, HEAD_DIM, HEAD_DIM),
            cs.reshape(1, dec_batch, CONV_W - 1, D_MODEL))
```

```python
import functools

import jax
import jax.numpy as jnp
from jax import lax
from jax.experimental import pallas as pl
from jax.experimental.pallas import tpu as pltpu

D_MODEL = 1024
HEADS = 8
HEAD_DIM = 128
CONV_W = 3
D_FF = 2816
PLE_DIM = 256
EPS = 1e-6
N_SECTIONS = 9
SIDE_SEC0 = 4
SUBLANES = 8
BF16_ROWS = 16
SEGMENT = 256
HALF = SEGMENT // 2
MIXER_BLOCK = 512
FFN_BLOCK = 1024
FFN_SUB = 256
VMEM_LIMIT_BYTES = 56 * 1024 * 1024

F32 = jnp.float32
BF16 = jnp.bfloat16


def _dot(a, b):
    return jnp.dot(a, b, preferred_element_type=F32)


def _dot_nt(a, b):
    return lax.dot_general(a, b, (((1,), (1,)), ((), ())), preferred_element_type=F32)


def _dot_tn(a, b):
    return lax.dot_general(a, b, (((0,), (0,)), ((), ())), preferred_element_type=F32)


def _rmsnorm(x, g):
    ms = jnp.mean(x * x, axis=-1, keepdims=True)
    return x * lax.rsqrt(ms + EPS) * g


def _sigmoid(x):
    return 1.0 / (1.0 + jnp.exp(-x))


def _level_step(c, blk):
    t, lanes = c.shape
    if blk >= SUBLANES:
        g = t // (2 * blk)
        c4 = c.reshape(g, 2, blk, lanes)
        ev = c4[:, 0]
        od = c4[:, 1]
        r = ev[:, blk - 1:blk, :]
        a = jnp.stack([r - ev, od], axis=1).reshape(t, lanes)
        cn = jnp.stack([ev, od + r], axis=1).reshape(t, lanes)
        return a, cn
    c3 = c.reshape(t // SUBLANES, SUBLANES, lanes)
    sub = lax.broadcasted_iota(jnp.int32, c3.shape, 1)
    if blk == 1:
        odd = (sub & 1) != 0
        prev = pltpu.roll(c3, 1, 1)
        a = jnp.where(odd, c3, 0.0).reshape(t, lanes)
        cn = jnp.where(odd, c3 + prev, c3).reshape(t, lanes)
        return a, cn
    r = None
    for grp in range(SUBLANES // (2 * blk)):
        row = grp * 2 * blk + blk - 1
        rg = jnp.broadcast_to(c3[:, row:row + 1, :], c3.shape)
        r = rg if r is None else jnp.where(sub >= grp * 2 * blk, rg, r)
    odd = (sub & blk) != 0
    a = jnp.where(odd, c3, r - c3).reshape(t, lanes)
    cn = jnp.where(odd, c3 + r, c3).reshape(t, lanes)
    return a, cn


def _level_codes(t, seg):
    row = lax.broadcasted_iota(jnp.int32, (t, t), 0)
    col = lax.broadcasted_iota(jnp.int32, (t, t), 1)
    x = row ^ col
    cnt = jnp.zeros((t, t), jnp.int32)
    blk = 1
    while blk < seg:
        cnt = cnt + jnp.where(x >= blk, 1, 0)
        blk *= 2
    lower = jnp.where(row > col, jnp.where(x < seg, cnt, -1), -1)
    return jnp.where(row == col, 0, lower)


def _mixer_kernel(*refs, t, nseq, nseg, has_init, nj):
    if has_init:
        (x_ref, s0_ref, c0_ref, lbw_ref, nmix_ref, win_ref, convw_ref, hgn_ref, wa_ref, wb_ref,
         wout_ref, y_ref, sout_ref, cout_ref,
         n_scr, z_scr, st_scr, code_scr) = refs
    else:
        (x_ref, lbw_ref, nmix_ref, win_ref, convw_ref, hgn_ref, wa_ref, wb_ref,
         wout_ref, y_ref, sout_ref, cout_ref,
         n_scr, z_scr, st_scr, code_scr) = refs
        s0_ref = c0_ref = None
    seg = t // nseg
    conv_len = t // nseq
    b = pl.program_id(0)
    j = pl.program_id(1)

    @pl.when((b == 0) & (j == 0))
    def _():
        code_scr[...] = _level_codes(HALF, seg).astype(F32).astype(BF16)

    @pl.when(j == 0)
    def _():
        for sq in range(nseq):
            for h in range(HEADS):
                if has_init:
                    st_scr[sq * HEADS + h] = s0_ref[0, sq, h].T
                else:
                    st_scr[sq * HEADS + h] = jnp.zeros((HEAD_DIM, HEAD_DIM), F32)
        if has_init:
            cout_ref[...] = c0_ref[...]
        else:
            cout_ref[...] = jnp.zeros(cout_ref.shape, F32)

    x = x_ref[0]
    n_scr[...] = _rmsnorm(x, nmix_ref[...]).astype(BF16)

    lbw = lbw_ref[...]
    mx = lbw[0:1]
    for r in range(1, lbw.shape[0]):
        mx = jnp.maximum(mx, lbw[r:r + 1])
    den = jnp.zeros_like(mx)
    for r in range(lbw.shape[0]):
        den = den + jnp.exp(lbw[r:r + 1] - mx)
    lb = jnp.exp(lbw[0:1] - mx) / den

    hgn = hgn_ref[...]
    code = code_scr[...]
    pair = 2 * HEAD_DIM

    def proj_cols(sec, c0):
        base = sec * D_MODEL + c0
        return _dot(n_scr[...], win_ref[:, base:base + pair])

    halves = [slice(hf * HALF, (hf + 1) * HALF) for hf in range(t // HALF)]

    n_pairs = HEADS // 2
    head_z = {}

    def head_proj(sec, pr):
        def run():
            head_z[sec, pr] = proj_cols(sec, pr * pair)
        return run

    def side_proj(sec, cb):
        def run():
            z_scr[sec - SIDE_SEC0, :, cb * pair:(cb + 1) * pair] = proj_cols(sec, cb * pair)
        return run

    fill_queue = []

    def fill():
        if fill_queue:
            fill_queue.pop(0)()

    def head(h, q, k, lf, v, sg):
        qb = q.astype(BF16)
        kb = k.astype(BF16)
        p = [jnp.where(code == 0, _dot_nt(qb[rows], kb[rows]).astype(BF16), 0.0)
             for rows in halves]
        p_low = {}
        c = lf
        blk, lvl = 1, 1
        while blk < seg:
            a, c = _level_step(c, blk)
            xs = jnp.exp2(a)
            xb = xs.astype(BF16)
            if blk == HALF:
                for hf in range(1, len(halves), 2):
                    hi, lo = halves[hf], halves[hf - 1]
                    p_low[hf] = _dot_nt(qb[hi] * xb[hi], kb[lo] * xb[lo]).astype(BF16)
            elif blk >= BF16_ROWS:
                g = HALF // (2 * blk)
                odd = lambda z: z.reshape(g, 2, blk, HEAD_DIM)[:, 1].reshape(g * blk, HEAD_DIM)
                kx = kb * xb
                for hf, rows in enumerate(halves):
                    qx = odd(qb[rows]) * odd(xb[rows])
                    s = _dot_nt(qx, kx[rows]).astype(BF16).reshape(g, blk, HALF)
                    p4 = p[hf].reshape(g, 2, blk, HALF)
                    lvl_mask = code.reshape(g, 2, blk, HALF)[:, 1] == lvl
                    merged = jnp.where(lvl_mask, s, p4[:, 1])
                    p[hf] = jnp.stack([p4[:, 0], merged], axis=1).reshape(HALF, HALF)
            else:
                qx = qb * xb
                kx = kb * xb
                for hf, rows in enumerate(halves):
                    p[hf] = jnp.where(code == lvl, _dot_nt(qx[rows], kx[rows]).astype(BF16), p[hf])
            blk *= 2
            lvl += 1
        o_parts = []
        for hf, rows in enumerate(halves):
            if hf in p_low:
                both = slice(halves[hf - 1].start, rows.stop)
                o_parts.append(_dot(jnp.concatenate([p_low[hf], p[hf]], axis=1), v[both]))
            else:
                o_parts.append(_dot(p[hf], v[rows]))
        o = jnp.concatenate(o_parts, axis=0)
        inter = []
        for si in range(nseg):
            rows = slice(si * seg, (si + 1) * seg)
            slot = (si if nseq > 1 else 0) * HEADS + h
            cs = c[rows]
            tot = cs[seg - 1:seg]
            st = st_scr[slot]
            qd = (q[rows] * jnp.exp2(cs)).astype(BF16)
            inter.append(_dot_nt(qd, st.astype(BF16)))
            kd = (k[rows] * jnp.exp2(tot - cs)).astype(BF16)
            st_scr[slot] = st * jnp.exp2(tot) + _dot_tn(v[rows], kd)
        o = o + (inter[0] if nseg == 1 else jnp.concatenate(inter, axis=0))
        ms = jnp.mean(o * o, axis=-1, keepdims=True)
        return (o * lax.rsqrt(ms + EPS * HEAD_DIM) * hgn * sg).astype(BF16)

    head_in = {}

    def head_prep(pr):
        def run():
            c0 = pr * pair
            zq = head_z.pop((0, pr))
            q2 = zq * _sigmoid(zq)
            sgf = _sigmoid(head_z.pop((1, pr)))
            lbp = lb[:, c0:c0 + pair]
            gap = (1.0 - lbp) * sgf
            lf2 = jnp.log2(lbp + gap)
            k2 = (1.0 - lbp) - gap
            v2 = head_z.pop((2, pr)).astype(BF16)
            zg = head_z.pop((3, pr))
            head_in[pr] = (q2, k2, lf2, v2, zg * _sigmoid(zg))
        return run

    oa_parts = []
    for sec in range(SIDE_SEC0):
        head_proj(sec, 0)()
    head_prep(0)()
    for pr in range(n_pairs):
        if pr + 1 < n_pairs:
            fill_queue += [head_proj(sec, pr + 1) for sec in range(SIDE_SEC0)]
            fill_queue.append(head_prep(pr + 1))
        fill_queue += [side_proj(sec, pr) for sec in range(SIDE_SEC0, N_SECTIONS)]
        pair_in = head_in.pop(pr)
        for hh in range(2):
            cols = slice(hh * HEAD_DIM, (hh + 1) * HEAD_DIM)
            oa_parts.append(head(2 * pr + hh, *(a[:, cols] for a in pair_in)))
        while fill_queue:
            fill()

    oa = jnp.concatenate(oa_parts, axis=1)
    ya = _dot(oa, wa_ref[...])

    bg = z_scr[0]
    u = z_scr[1] * z_scr[2]
    cw = convw_ref[...]
    rowi = lax.broadcasted_iota(jnp.int32, (conv_len, D_MODEL), 0)
    conv_parts = []
    for sq in range(nseq):
        us = u[sq * conv_len:(sq + 1) * conv_len]
        prev = cout_ref[0, sq]
        s1 = jnp.where(rowi == 0, prev[1:2], pltpu.roll(us, 1, 0))
        s2 = jnp.where(rowi == 0, prev[0:1], jnp.where(rowi == 1, prev[1:2], pltpu.roll(us, 2, 0)))
        conv_parts.append(cw[0:1] * s2 + cw[1:2] * s1 + cw[2:3] * us)
        cout_ref[0, sq] = us[conv_len - (CONV_W - 1):conv_len]
    conv = conv_parts[0] if nseq == 1 else jnp.concatenate(conv_parts, axis=0)
    ob = (bg * conv).astype(BF16)

    mix = _sigmoid(z_scr[3]) * ya + _sigmoid(z_scr[4]) * _dot(ob, wb_ref[...])
    y_ref[0] = x + _dot(mix.astype(BF16), wout_ref[...])

    @pl.when(j == nj - 1)
    def _():
        for sq in range(nseq):
            for h in range(HEADS):
                sout_ref[0, sq, h] = st_scr[sq * HEADS + h].T


def _ffn_kernel(x_ref, p_ref, nffn_ref, wgu_ref, wdown_ref, nple_ref, wple_ref, wpg_ref, nfin_ref,
                y_ref):
    subs = [pl.ds(i * FFN_SUB, FFN_SUB) for i in range(x_ref.shape[0] // FFN_SUB)]
    xs = [x_ref[r, :] for r in subs]
    ns = [_rmsnorm(x, nffn_ref[...]).astype(BF16) for x in xs]
    hidden = []
    for n in ns:
        gate = _dot(n, wgu_ref[:, 0:D_FF])
        up = _dot(n, wgu_ref[:, D_FF:2 * D_FF])
        hidden.append((gate * _sigmoid(gate) * up).astype(BF16))
    xs = [x + _dot(h, wdown_ref[...]) for x, h in zip(xs, hidden)]
    ples = [_dot(p_ref[r, :].astype(BF16), wple_ref[...]) for r in subs]
    n2s = [_rmsnorm(x, nple_ref[...]).astype(BF16) for x in xs]
    xs = [x + _sigmoid(_dot(n2, wpg_ref[...])) * ple for x, n2, ple in zip(xs, n2s, ples)]
    for r, x in zip(subs, xs):
        y_ref[r, :] = _rmsnorm(x, nfin_ref[...])


def _const_spec(shape, grid_rank=2):
    zeros = (0,) * len(shape)
    index_map = (lambda i: zeros) if grid_rank == 1 else (lambda b, j: zeros)
    return pl.BlockSpec(shape, index_map, pipeline_mode=pl.Buffered(1))


def _mixer_call(x, s0, c0, weights, *, t, nseq, nseg):
    nb, tokens, _ = x.shape
    nj = tokens // t
    has_init = s0 is not None
    assert tokens % t == 0 and (nseq == 1 or nj == 1) and (nseq == 1 or nseq == nseg)
    assert (t // nseg) in (SEGMENT, SEGMENT // 4) and t % HALF == 0
    state_spec = pl.BlockSpec((1, nseq, HEADS, HEAD_DIM, HEAD_DIM), lambda b, j: (b, 0, 0, 0, 0))
    conv_spec = pl.BlockSpec((1, nseq, CONV_W - 1, D_MODEL), lambda b, j: (b, 0, 0, 0))
    x_spec = pl.BlockSpec((1, t, D_MODEL), lambda b, j: (b, j, 0))
    in_specs = [x_spec]
    args = [x]
    if has_init:
        in_specs += [state_spec, conv_spec]
        args += [s0, c0]
    for w in weights:
        in_specs.append(_const_spec(w.shape))
        args.append(w)
    kern = functools.partial(_mixer_kernel, t=t, nseq=nseq, nseg=nseg, has_init=has_init, nj=nj)
    return pl.pallas_call(
        kern,
        grid=(nb, nj),
        in_specs=in_specs,
        out_specs=[x_spec, state_spec, conv_spec],
        out_shape=[
            jax.ShapeDtypeStruct(x.shape, F32),
            jax.ShapeDtypeStruct((nb, nseq, HEADS, HEAD_DIM, HEAD_DIM), F32),
            jax.ShapeDtypeStruct((nb, nseq, CONV_W - 1, D_MODEL), F32),
        ],
        scratch_shapes=[
            pltpu.VMEM((t, D_MODEL), BF16),
            pltpu.VMEM((N_SECTIONS - SIDE_SEC0, t, D_MODEL), F32),
            pltpu.VMEM((nseq * HEADS, HEAD_DIM, HEAD_DIM), F32),
            pltpu.VMEM((HALF, HALF), BF16),
        ],
        compiler_params=pltpu.CompilerParams(
            dimension_semantics=("arbitrary", "arbitrary"),
            vmem_limit_bytes=VMEM_LIMIT_BYTES),
        name="mixer_init" if has_init else "mixer_zero",
    )(*args)


def _ffn_call(x, p, weights, name):
    tokens = x.shape[0]
    t = FFN_BLOCK if tokens >= 4 * FFN_BLOCK else FFN_SUB
    assert tokens % t == 0 and t % FFN_SUB == 0
    x_spec = pl.BlockSpec((t, D_MODEL), lambda i: (i, 0))
    p_spec = pl.BlockSpec((t, PLE_DIM), lambda i: (i, 0))
    return pl.pallas_call(
        _ffn_kernel,
        grid=(tokens // t,),
        in_specs=[x_spec, p_spec] + [_const_spec(w.shape, 1) for w in weights],
        out_specs=x_spec,
        out_shape=jax.ShapeDtypeStruct(x.shape, F32),
        compiler_params=pltpu.CompilerParams(
            dimension_semantics=("arbitrary",),
            vmem_limit_bytes=VMEM_LIMIT_BYTES),
        name=name,
    )(x, p, *weights)


def kernel(x_prompt, x_sample, p_prompt, p_sample, state_hgrn, state_conv, lower_bounds, norm_mix, w_in, conv_w, hg_norm, w_branch_a, w_branch_b, w_out, norm_ffn, w_gate_up, w_down, norm_ple, w_ple, w_ple_gate, norm_final):
    depth = w_in.shape[0]
    assert depth == 1
    batch, seq, _ = x_prompt.shape
    dec_batch, dec_seq, _ = x_sample.shape
    assert seq % MIXER_BLOCK == 0 and SEGMENT % dec_seq == 0
    spb = SEGMENT // dec_seq
    assert dec_batch % spb == 0
    nsb = dec_batch // spb

    row = lambda a: a.reshape(1, -1)
    mixer_w = [lower_bounds, row(norm_mix[0]), w_in[0].astype(BF16), conv_w[0], row(hg_norm[0]),
               w_branch_a[0].astype(BF16), w_branch_b[0].astype(BF16), w_out[0].astype(BF16)]
    ffn_w = [row(norm_ffn[0]), w_gate_up[0].astype(BF16), w_down[0].astype(BF16), row(norm_ple[0]),
             w_ple[0].astype(BF16), w_ple_gate[0].astype(BF16), row(norm_final)]

    hp, sp, cp = _mixer_call(x_prompt, None, None, mixer_w, t=MIXER_BLOCK, nseq=1,
                             nseg=MIXER_BLOCK // SEGMENT)
    xs = x_sample.reshape(nsb, SEGMENT, D_MODEL)
    s0 = state_hgrn[0].reshape(nsb, spb, HEADS, HEAD_DIM, HEAD_DIM)
    c0 = state_conv[0].reshape(nsb, spb, CONV_W - 1, D_MODEL)
    hs, ss, cs = _mixer_call(xs, s0, c0, mixer_w, t=SEGMENT, nseq=spb, nseg=spb)

    y_prompt = _ffn_call(hp.reshape(-1, D_MODEL), p_prompt[0].reshape(-1, PLE_DIM), ffn_w,
                         "ffn_prompt")
    y_sample = _ffn_call(hs.reshape(-1, D_MODEL), p_sample[0].reshape(-1, PLE_DIM), ffn_w,
                         "ffn_sample")

    return (y_prompt.reshape(batch, seq, D_MODEL),
            y_sample.reshape(dec_batch, dec_seq, D_MODEL),
            sp.reshape(1, batch, HEADS, HEAD_DIM, HEAD_DIM),
            cp.reshape(1, batch, CONV_W - 1, D_MODEL),
            ss.reshape(1, dec_batch, HEADS, HEAD_DIM, HEAD_DIM),
            cs.reshape(1, dec_batch, CONV_W - 1, D_MODEL))
```
